```python
import jax, jax.numpy as jnp
from jax import lax
import numpy as np

D_MODEL = 2048
BATCH = 4
SEQ = 4096
DEPTH = 2

GRID_W = 64
CTX_LEN = 256
F_WIDTH = 1024
F_GROUPS = 8
F_GROUP_DIM = F_WIDTH // F_GROUPS
C_WIDTH = 1024
CONV_K = 31
N_HEADS = 16
QK_NOPE = 128
QK_ROPE = 64
V_DIM = 128
Q_LORA = 768
KV_LORA = 512
ROPE_BASE = 10000.0
ATTN_SCALE = (QK_NOPE + QK_ROPE) ** -0.5
Q_BLOCK = 128
N_BRANCH = 3
FFN_HIDDEN = -(-8 * D_MODEL // (3 * 256)) * 256
EPS = 1e-6
F_END = F_WIDTH
C_END = F_END + 2 * C_WIDTH
Q_END = C_END + Q_LORA
KV_END = Q_END + KV_LORA
KPE_END = KV_END + QK_ROPE
IN_COLS = KPE_END + N_BRANCH * D_MODEL

kernel_name = "hybrid_fnet_conformer_mla_dit_block"


def rmsnorm(x, g):
    xf = x.astype(jnp.float32)
    y = xf * lax.rsqrt(jnp.mean(xf * xf, axis=-1, keepdims=True) + EPS)
    return (y * g.astype(jnp.float32)).astype(x.dtype)


def layernorm(x, g, b):
    xf = x.astype(jnp.float32)
    mu = jnp.mean(xf, axis=-1, keepdims=True)
    var = jnp.mean(jnp.square(xf - mu), axis=-1, keepdims=True)
    y = (xf - mu) * lax.rsqrt(var + EPS)
    return (y * g.astype(jnp.float32) + b.astype(jnp.float32)).astype(x.dtype)


def modulate(h, shift, scale):
    return h * (1 + scale[:, None, :]) + shift[:, None, :]


def rope_2d(n_tok):
    rows = n_tok // GRID_W
    row = jnp.broadcast_to(jnp.arange(rows, dtype=jnp.float32)[:, None], (rows, GRID_W)).reshape(-1)
    col = jnp.broadcast_to(jnp.arange(GRID_W, dtype=jnp.float32)[None, :], (rows, GRID_W)).reshape(-1)
    n_freq = QK_ROPE // 4
    inv_freq = ROPE_BASE ** (-jnp.arange(n_freq, dtype=jnp.float32) / n_freq)
    ang = jnp.concatenate([row[:, None] * inv_freq, col[:, None] * inv_freq], axis=-1)
    return jnp.cos(ang), jnp.sin(ang)


def apply_rope(x, cos, sin):
    half = QK_ROPE // 2
    xf = x.astype(jnp.float32)
    x1, x2 = xf[..., :half], xf[..., half:]
    return jnp.concatenate([x1 * cos - x2 * sin, x1 * sin + x2 * cos], axis=-1).astype(x.dtype)


def fourier_mix(u):
    b, n, _ = u.shape
    uf = u.astype(jnp.float32).reshape(b, n, F_GROUPS, F_GROUP_DIM)
    y = jnp.fft.fft2(uf, axes=(1, 3), norm="ortho").real
    return y.reshape(b, n, F_WIDTH).astype(u.dtype)


def conformer_conv(u, lp):
    a, g = u[..., :C_WIDTH], u[..., C_WIDTH:]
    v = a * jax.nn.sigmoid(g)
    v = lax.conv_general_dilated(
        v, lp["conv_w"][:, None, :], window_strides=(1,),
        padding=[(CONV_K // 2, CONV_K // 2)],
        dimension_numbers=("NWC", "WIO", "NWC"),
        feature_group_count=C_WIDTH) + lp["conv_b"]
    v = jax.nn.silu(layernorm(v, lp["conv_ln_g"], lp["conv_ln_b"]))
    return v @ lp["w_conv_out"]


def mla_query(q_down, lp, rope):
    b, n, _ = q_down.shape
    q = (rmsnorm(q_down, lp["q_norm_g"]) @ lp["w_uq"]).reshape(b, n, N_HEADS, QK_NOPE + QK_ROPE)
    q_nope, q_pe = q[..., :QK_NOPE], q[..., QK_NOPE:]
    if rope is not None:
        q_pe = apply_rope(q_pe, rope[0][:, None, :], rope[1][:, None, :])
    return q_nope, q_pe


def mla_keys(kv_down, k_pe, lp, rope):
    b, n, _ = kv_down.shape
    kv = (rmsnorm(kv_down, lp["kv_norm_g"]) @ lp["w_ukv"]).reshape(b, n, N_HEADS, QK_NOPE + V_DIM)
    k_nope, v = kv[..., :QK_NOPE], kv[..., QK_NOPE:]
    if rope is not None:
        k_pe = apply_rope(k_pe, rope[0], rope[1])
    return k_nope, k_pe, v


def attend(q_nope, q_pe, k_nope, k_pe, v):
    s = (jnp.einsum("bqhd,bkhd->bhqk", q_nope, k_nope)
         + jnp.einsum("bqhd,bkd->bhqk", q_pe, k_pe)).astype(jnp.float32) * ATTN_SCALE
    p = jax.nn.softmax(s, axis=-1).astype(v.dtype)
    return jnp.einsum("bhqk,bkhd->bqhd", p, v)


def blocked_attention(q_nope, q_pe, k_nope, k_pe, v):
    b, n = q_nope.shape[:2]
    nb = n // Q_BLOCK
    qn = q_nope.reshape(b, nb, Q_BLOCK, N_HEADS, QK_NOPE).swapaxes(0, 1)
    qp = q_pe.reshape(b, nb, Q_BLOCK, N_HEADS, QK_ROPE).swapaxes(0, 1)
    o = lax.map(lambda qs: attend(qs[0], qs[1], k_nope, k_pe, v), (qn, qp))
    return o.swapaxes(0, 1).reshape(b, n, N_HEADS * V_DIM)


def merge_branches(proj, attn_o, lp):
    y_f = fourier_mix(proj[..., :F_END]) @ lp["w_fourier"]
    y_c = conformer_conv(proj[..., F_END:C_END], lp)
    y_a = attn_o @ lp["w_mla_o"]
    g = jax.nn.sigmoid(proj[..., KPE_END:] + lp["b_gate"])
    g_f, g_c, g_a = jnp.split(g, N_BRANCH, axis=-1)
    return (g_f * y_f + g_c * y_c + g_a * y_a) @ lp["w_out"]


def swiglu(h, lp):
    return (jax.nn.silu(h @ lp["w_ffn_gate"]) * (h @ lp["w_ffn_up"])) @ lp["w_ffn_down"]


def setup_inputs(seed: int = 0) -> dict:
    key = jax.random.key(seed)
    ks = jax.random.split(key, 26)

    def nrm(k, shape, scale):
        return jax.random.normal(k, shape, jnp.float32) * scale

    def gain(k, shape):
        return 1.0 + 0.02 * jax.random.normal(k, shape, jnp.float32)

    return {
        "x": nrm(ks[0], (BATCH, SEQ, D_MODEL), 1.0),
        "c": nrm(ks[1], (BATCH, D_MODEL), 1.0),
        "ctx": nrm(ks[2], (BATCH, CTX_LEN, D_MODEL), 1.0),
        "c_ctx": nrm(ks[3], (D_MODEL,), 1.0),
        "ada_w": nrm(ks[4], (DEPTH, D_MODEL, 6 * D_MODEL), 0.5 * D_MODEL ** -0.5),
        "ada_b": nrm(ks[5], (DEPTH, 6 * D_MODEL), 0.02),
        "norm_mix_g": gain(ks[6], (DEPTH, D_MODEL)),
        "w_in": nrm(ks[7], (DEPTH, D_MODEL, IN_COLS), D_MODEL ** -0.5),
        "b_gate": nrm(ks[8], (DEPTH, N_BRANCH * D_MODEL), 0.02),
        "w_fourier": nrm(ks[9], (DEPTH, F_WIDTH, D_MODEL), F_WIDTH ** -0.5),
        "conv_w": nrm(ks[10], (DEPTH, CONV_K, C_WIDTH), CONV_K ** -0.5),
        "conv_b": nrm(ks[11], (DEPTH, C_WIDTH), 0.02),
        "conv_ln_g": gain(ks[12], (DEPTH, C_WIDTH)),
        "conv_ln_b": nrm(ks[13], (DEPTH, C_WIDTH), 0.02),
        "w_conv_out": nrm(ks[14], (DEPTH, C_WIDTH, D_MODEL), C_WIDTH ** -0.5),
        "q_norm_g": gain(ks[15], (DEPTH, Q_LORA)),
        "w_uq": nrm(ks[16], (DEPTH, Q_LORA, N_HEADS * (QK_NOPE + QK_ROPE)), Q_LORA ** -0.5),
        "kv_norm_g": gain(ks[17], (DEPTH, KV_LORA)),
        "w_ukv": nrm(ks[18], (DEPTH, KV_LORA, N_HEADS * (QK_NOPE + V_DIM)), KV_LORA ** -0.5),
        "w_mla_o": nrm(ks[19], (DEPTH, N_HEADS * V_DIM, D_MODEL), (N_HEADS * V_DIM) ** -0.5),
        "w_out": nrm(ks[20], (DEPTH, D_MODEL, D_MODEL), D_MODEL ** -0.5),
        "norm_ffn_g": gain(ks[21], (DEPTH, D_MODEL)),
        "w_ffn_gate": nrm(ks[22], (DEPTH, D_MODEL, FFN_HIDDEN), D_MODEL ** -0.5),
        "w_ffn_up": nrm(ks[23], (DEPTH, D_MODEL, FFN_HIDDEN), D_MODEL ** -0.5),
        "w_ffn_down": nrm(ks[24], (DEPTH, FFN_HIDDEN, D_MODEL), FFN_HIDDEN ** -0.5),
        "final_norm_g": gain(ks[25], (D_MODEL,)),
    }


def reference(x, c, ctx, c_ctx, ada_w, ada_b, norm_mix_g, w_in, b_gate, w_fourier,
              conv_w, conv_b, conv_ln_g, conv_ln_b, w_conv_out, q_norm_g, w_uq,
              kv_norm_g, w_ukv, w_mla_o, w_out, norm_ffn_g, w_ffn_gate, w_ffn_up,
              w_ffn_down, final_norm_g):
    rope = rope_2d(x.shape[1])
    b = x.shape[0]
    for i in range(DEPTH):
        last = i == DEPTH - 1
        lp = dict(b_gate=b_gate[i], w_fourier=w_fourier[i], conv_w=conv_w[i], conv_b=conv_b[i],
                  conv_ln_g=conv_ln_g[i], conv_ln_b=conv_ln_b[i], w_conv_out=w_conv_out[i],
                  q_norm_g=q_norm_g[i], w_uq=w_uq[i], kv_norm_g=kv_norm_g[i], w_ukv=w_ukv[i],
                  w_mla_o=w_mla_o[i], w_out=w_out[i], w_ffn_gate=w_ffn_gate[i],
                  w_ffn_up=w_ffn_up[i], w_ffn_down=w_ffn_down[i])
        mod = jax.nn.silu(c) @ ada_w[i] + ada_b[i]
        mod_c = jax.nn.silu(c_ctx)[None, :] @ ada_w[i] + ada_b[i]
        sh1, sc1, g1, sh2, sc2, g2 = jnp.split(mod, 6, axis=-1)
        csh1, csc1, cg1, csh2, csc2, cg2 = jnp.split(mod_c, 6, axis=-1)

        hx = modulate(rmsnorm(x, norm_mix_g[i]), sh1, sc1)
        hc = modulate(rmsnorm(ctx, norm_mix_g[i]), csh1, csc1)
        px = hx @ w_in[i]
        if last:
            pkv = hc @ w_in[i][:, Q_END:KPE_END]
            kv_c, kpe_c = pkv[..., :KV_LORA], pkv[..., KV_LORA:]
        else:
            pc = hc @ w_in[i]
            kv_c, kpe_c = pc[..., Q_END:KV_END], pc[..., KV_END:KPE_END]

        qn_x, qp_x = mla_query(px[..., C_END:Q_END], lp, rope)
        kn_x, kp_x, v_x = mla_keys(px[..., Q_END:KV_END], px[..., KV_END:KPE_END], lp, rope)
        kn_c, kp_c, v_c = mla_keys(kv_c, kpe_c, lp, None)
        o_x = blocked_attention(qn_x, qp_x,
                                jnp.concatenate([kn_x, kn_c], axis=1),
                                jnp.concatenate([kp_x, kp_c], axis=1),
                                jnp.concatenate([v_x, v_c], axis=1))
        x_mixed = x + g1[:, None, :] * merge_branches(px, o_x, lp)

        if not last:
            qn_c, qp_c = mla_query(pc[..., C_END:Q_END], lp, None)
            o_c = attend(qn_c, qp_c, kn_c, kp_c, v_c).reshape(b, ctx.shape[1], N_HEADS * V_DIM)
            ctx = ctx + cg1[:, None, :] * merge_branches(pc, o_c, lp)
            ctx = ctx + cg2[:, None, :] * swiglu(modulate(rmsnorm(ctx, norm_ffn_g[i]), csh2, csc2), lp)

        x = x_mixed + g2[:, None, :] * swiglu(modulate(rmsnorm(x_mixed, norm_ffn_g[i]), sh2, sc2), lp)
    return rmsnorm(x, final_norm_g)
```

```python
import functools

import jax
import jax.numpy as jnp
from jax import lax
from jax.experimental import pallas as pl
from jax.experimental.pallas import tpu as pltpu

F32 = jnp.float32
BF16 = jnp.bfloat16

GRID_W = 64
N_HEADS = 16
QK_NOPE = 128
QK_ROPE = 64
V_DIM = 128
F_GROUPS = 8
F_GROUP_DIM = 128
CONV_K = 31
ROPE_BASE = 10000.0
ATTN_SCALE = (QK_NOPE + QK_ROPE) ** -0.5
EPS = 1e-6

LANES = 128
SUBLANES = 8
BF16_ROWS = 16
VMEM_LIMIT = 56 * 1024 * 1024

HEAD_W = 2 * LANES
HALO = BF16_ROWS

PX_F = 0
PX_A = 1024
PX_G = 2048
PX_KVD = 3072
PX_KPE = 3584
PX_QD = 3840
PX_GATE = 4608
PX_COLS = 10752
PKV_KVD = 0
PKV_KPE = 512
PKV_COLS = 768


def _cparams(n_axes):
    return pltpu.CompilerParams(
        dimension_semantics=("arbitrary",) * n_axes, vmem_limit_bytes=VMEM_LIMIT)


def _blk(off, width):
    assert off % width == 0, (off, width)
    return off // width


def _dot(a, b):
    return jnp.dot(a, b, preferred_element_type=F32)


def _silu(v):
    return v * jax.nn.sigmoid(v)


def _rms(x, g):
    r = lax.rsqrt(jnp.mean(x * x, axis=-1, keepdims=True) + EPS)
    return (x * r) * g


def _rope128(pe, cos, sin):
    lane = lax.broadcasted_iota(jnp.int32, pe.shape, 1)
    half = QK_ROPE // 2
    rot = jnp.where(lane < half, -pltpu.roll(pe, LANES - half, 1), pltpu.roll(pe, half, 1))
    return pe * cos + rot * sin


def _ada_kernel(c_ref, w_ref, b_ref, o_ref):
    a = _silu(c_ref[...]).astype(BF16)
    o_ref[...] = _dot(a, w_ref[...].astype(BF16)) + b_ref[...]


def _ada(c8, ada_w, ada_b):
    depth, d, n6 = ada_w.shape
    tn = 1024
    return pl.pallas_call(
        _ada_kernel,
        grid=(depth, n6 // tn),
        in_specs=[
            pl.BlockSpec((SUBLANES, d), lambda l, j: (0, 0)),
            pl.BlockSpec((None, d, tn), lambda l, j: (l, 0, j)),
            pl.BlockSpec((None, 1, tn), lambda l, j: (l, 0, j)),
        ],
        out_specs=pl.BlockSpec((None, SUBLANES, tn), lambda l, j: (l, 0, j)),
        out_shape=jax.ShapeDtypeStruct((depth, SUBLANES, n6), F32),
        compiler_params=_cparams(2),
        name="ada_mod",
    )(c8, ada_w, ada_b.reshape(depth, 1, n6))


class _Rows:
    def __init__(self, m, rows_per_batch, fixed_mod_row):
        self.m = m
        self.rows_per_batch = rows_per_batch
        self.fixed_mod_row = fixed_mod_row

    def tile(self, preferred):
        return min(preferred, self.rows_per_batch or self.m)

    def mod_row(self, tm):
        if self.fixed_mod_row is not None:
            row = self.fixed_mod_row
            return lambda i: row
        rpb = self.rows_per_batch
        assert rpb % tm == 0
        return lambda i: (i * tm) // rpb


def _mod_spec(rows, tm, d, chunk):
    row = rows.mod_row(tm)
    return pl.BlockSpec((None, 1, d), lambda i, j: (row(i), 0, chunk))


def _inproj_kernel(x_ref, g_ref, sc_ref, sh_ref, w_ref, o_ref, h_ref):
    @pl.when(pl.program_id(1) == 0)
    def _():
        h = _rms(x_ref[...], g_ref[...]) * (1.0 + sc_ref[...]) + sh_ref[...]
        h_ref[...] = h.astype(BF16)

    o_ref[...] = _dot(h_ref[...], w_ref[...]).astype(o_ref.dtype)


def _inproj(x, g, mod3, w, rows, *, shift_chunk, tn):
    m, d = x.shape
    n = w.shape[1]
    tm = rows.tile(1024)
    return pl.pallas_call(
        _inproj_kernel,
        grid=(m // tm, n // tn),
        in_specs=[
            pl.BlockSpec((tm, d), lambda i, j: (i, 0)),
            pl.BlockSpec((1, d), lambda i, j: (0, 0)),
            _mod_spec(rows, tm, d, shift_chunk + 1),
            _mod_spec(rows, tm, d, shift_chunk),
            pl.BlockSpec((d, tn), lambda i, j: (0, j)),
        ],
        out_specs=pl.BlockSpec((tm, tn), lambda i, j: (i, j)),
        out_shape=jax.ShapeDtypeStruct((m, n), BF16),
        scratch_shapes=[pltpu.VMEM((tm, d), BF16)],
        compiler_params=_cparams(2),
        name="norm_mod_proj",
    )(x, g.reshape(1, d), mod3, mod3, w)


def _ffn_up_kernel(x_ref, g_ref, sc_ref, sh_ref, wg_ref, wu_ref, o_ref, h_ref):
    @pl.when(pl.program_id(1) == 0)
    def _():
        h = _rms(x_ref[...], g_ref[...]) * (1.0 + sc_ref[...]) + sh_ref[...]
        h_ref[...] = h.astype(BF16)

    hb = h_ref[...]
    o_ref[...] = (_silu(_dot(hb, wg_ref[...])) * _dot(hb, wu_ref[...])).astype(o_ref.dtype)


def _ffn_up(x, g, mod3, wg, wu, rows):
    m, d = x.shape
    n = wg.shape[1]
    tm = rows.tile(1024)
    tn = 512
    return pl.pallas_call(
        _ffn_up_kernel,
        grid=(m // tm, n // tn),
        in_specs=[
            pl.BlockSpec((tm, d), lambda i, j: (i, 0)),
            pl.BlockSpec((1, d), lambda i, j: (0, 0)),
            _mod_spec(rows, tm, d, 4),
            _mod_spec(rows, tm, d, 3),
            pl.BlockSpec((d, tn), lambda i, j: (0, j)),
            pl.BlockSpec((d, tn), lambda i, j: (0, j)),
        ],
        out_specs=pl.BlockSpec((tm, tn), lambda i, j: (i, j)),
        out_shape=jax.ShapeDtypeStruct((m, n), BF16),
        scratch_shapes=[pltpu.VMEM((tm, d), BF16)],
        compiler_params=_cparams(2),
        name="ffn_gate_up",
    )(x, g.reshape(1, d), mod3, mod3, wg, wu)


def _resmm_kernel(a_ref, w_ref, res_ref, gate_ref, o_ref):
    o_ref[...] = res_ref[...] + gate_ref[...] * _dot(a_ref[...], w_ref[...])


def _resmm(a, w, res, mod3, rows, *, gate_chunk, tm, tn):
    m, k = a.shape
    d = w.shape[1]
    tm = rows.tile(tm)
    row = rows.mod_row(tm)
    per_chunk = d // tn
    return pl.pallas_call(
        _resmm_kernel,
        grid=(m // tm, d // tn),
        in_specs=[
            pl.BlockSpec((tm, k), lambda i, j: (i, 0)),
            pl.BlockSpec((k, tn), lambda i, j: (0, j)),
            pl.BlockSpec((tm, tn), lambda i, j: (i, j)),
            pl.BlockSpec((None, 1, tn), lambda i, j: (row(i), 0, gate_chunk * per_chunk + j)),
        ],
        out_specs=pl.BlockSpec((tm, tn), lambda i, j: (i, j)),
        out_shape=jax.ShapeDtypeStruct((m, d), F32),
        compiler_params=_cparams(2),
        name="proj_gated_residual",
    )(a, w, res, mod3)


def _qup_kernel(qd_ref, g_ref, cos_ref, sin_ref, w_ref, o_ref, h_ref, *, heads_per_tile):
    @pl.when(pl.program_id(1) == 0)
    def _():
        h_ref[...] = _rms(qd_ref[...].astype(F32), g_ref[...]).astype(BF16)

    acc = _dot(h_ref[...], w_ref[...]) * ATTN_SCALE
    cos = cos_ref[...]
    sin = sin_ref[...]
    for hh in range(heads_per_tile):
        c0 = hh * HEAD_W
        o_ref[:, c0:c0 + LANES] = acc[:, c0:c0 + LANES].astype(o_ref.dtype)
        pe = acc[:, c0 + LANES:c0 + HEAD_W]
        o_ref[:, c0 + LANES:c0 + HEAD_W] = _rope128(pe, cos, sin).astype(o_ref.dtype)


def _qup(p, g, cos, sin, w, *, qd_off):
    m = p.shape[0]
    kq, n = w.shape
    tm = min(1024, m)
    tn = 1024
    return pl.pallas_call(
        functools.partial(_qup_kernel, heads_per_tile=tn // HEAD_W),
        grid=(m // tm, n // tn),
        in_specs=[
            pl.BlockSpec((tm, kq), lambda i, j: (i, _blk(qd_off, kq))),
            pl.BlockSpec((1, kq), lambda i, j: (0, 0)),
            pl.BlockSpec((tm, LANES), lambda i, j: (i, 0)),
            pl.BlockSpec((tm, LANES), lambda i, j: (i, 0)),
            pl.BlockSpec((kq, tn), lambda i, j: (0, j)),
        ],
        out_specs=pl.BlockSpec((tm, tn), lambda i, j: (i, j)),
        out_shape=jax.ShapeDtypeStruct((m, n), BF16),
        scratch_shapes=[pltpu.VMEM((tm, kq), BF16)],
        compiler_params=_cparams(2),
        name="mla_q_up",
    )(p, g.reshape(1, kq), cos, sin, w)


def _kvup_kernel(kvd_ref, g_ref, kpe_ref, cos_ref, sin_ref, w_ref, o_ref, okpe_ref, h_ref):
    @pl.when(pl.program_id(1) == 0)
    def _():
        h_ref[...] = _rms(kvd_ref[...].astype(F32), g_ref[...]).astype(BF16)
        pe = kpe_ref[...].astype(F32)
        okpe_ref[...] = _rope128(pe, cos_ref[...], sin_ref[...]).astype(okpe_ref.dtype)

    o_ref[...] = _dot(h_ref[...], w_ref[...]).astype(o_ref.dtype)


def _kvup(p, g, cos, sin, w, *, kvd_off, kpe_off):
    m = p.shape[0]
    kk, n = w.shape
    tm = min(1024, m)
    tn = 1024
    return pl.pallas_call(
        _kvup_kernel,
        grid=(m // tm, n // tn),
        in_specs=[
            pl.BlockSpec((tm, kk), lambda i, j: (i, _blk(kvd_off, kk))),
            pl.BlockSpec((1, kk), lambda i, j: (0, 0)),
            pl.BlockSpec((tm, LANES), lambda i, j: (i, _blk(kpe_off, LANES))),
            pl.BlockSpec((tm, LANES), lambda i, j: (i, 0)),
            pl.BlockSpec((tm, LANES), lambda i, j: (i, 0)),
            pl.BlockSpec((kk, tn), lambda i, j: (0, j)),
        ],
        out_specs=[
            pl.BlockSpec((tm, tn), lambda i, j: (i, j)),
            pl.BlockSpec((tm, LANES), lambda i, j: (i, 0)),
        ],
        out_shape=[
            jax.ShapeDtypeStruct((m, n), BF16),
            jax.ShapeDtypeStruct((m, LANES), BF16),
        ],
        scratch_shapes=[pltpu.VMEM((tm, kk), BF16)],
        compiler_params=_cparams(2),
        name="mla_kv_up",
    )(p, g.reshape(1, kk), p, cos, sin, w)


def _attn_kernel(*refs, n_lat, n_ctx, tk):
    if n_lat:
        (q_ref, kl_ref, kpl_ref, vl_ref, kc_ref, kpc_ref, vc_ref,
         o_ref, k_scr, v_scr, m_scr, l_scr, acc_scr) = refs
    else:
        (q_ref, kc_ref, kpc_ref, vc_ref,
         o_ref, k_scr, v_scr, m_scr, l_scr, acc_scr) = refs
    n_keys = n_lat + n_ctx

    @pl.when(pl.program_id(2) == 0)
    def _():
        if n_lat:
            k_scr[0:n_lat, 0:LANES] = kl_ref[...]
            k_scr[0:n_lat, LANES:HEAD_W] = kpl_ref[...]
            v_scr[0:n_lat, :] = vl_ref[...]
        k_scr[n_lat:n_keys, 0:LANES] = kc_ref[...]
        k_scr[n_lat:n_keys, LANES:HEAD_W] = kpc_ref[...]
        v_scr[n_lat:n_keys, :] = vc_ref[...]

    m_scr[...] = jnp.full(m_scr.shape, -jnp.inf, F32)
    l_scr[...] = jnp.zeros(l_scr.shape, F32)
    acc_scr[...] = jnp.zeros(acc_scr.shape, F32)
    q = q_ref[...]

    def step(start, size):
        k = k_scr[pl.ds(start, size), :]
        v = v_scr[pl.ds(start, size), :]
        s = lax.dot_general(q, k, (((1,), (1,)), ((), ())), preferred_element_type=F32)
        m_prev = m_scr[...]
        m_new = jnp.maximum(m_prev, jnp.max(s, axis=-1, keepdims=True))
        alpha = jnp.exp(m_prev - m_new)
        p = jnp.exp(s - m_new)
        l_scr[...] = alpha * l_scr[...] + jnp.sum(p, axis=-1, keepdims=True)
        acc_scr[...] = alpha * acc_scr[...] + _dot(p.astype(BF16), v)
        m_scr[...] = m_new

    n_full = n_keys // tk
    rem = n_keys - n_full * tk
    if n_full == 1:
        step(0, tk)
    elif n_full > 1:
        def body(c, carry):
            step(pl.multiple_of(c * tk, tk), tk)
            return carry
        lax.fori_loop(0, n_full, body, 0)
    if rem:
        step(n_full * tk, rem)
    o_ref[...] = (acc_scr[...] / l_scr[...]).astype(o_ref.dtype)


def _attention(q, kv_c, kpe_c, *, n_batch, n_ctx, q_rows_per_batch, lat=None):
    m = q.shape[0]
    nq_rows = q_rows_per_batch
    tq = min(512, nq_rows)
    nq = nq_rows // tq
    n_lat = 0
    in_specs = [pl.BlockSpec((tq, HEAD_W), lambda b, h, i: (b * nq + i, h))]
    args = [q]
    if lat is not None:
        kv_l, kpe_l = lat
        n_lat = kv_l.shape[0] // n_batch
        in_specs += [
            pl.BlockSpec((n_lat, LANES), lambda b, h, i: (b, h)),
            pl.BlockSpec((n_lat, LANES), lambda b, h, i: (b, 0)),
            pl.BlockSpec((n_lat, LANES), lambda b, h, i: (b, N_HEADS + h)),
        ]
        args += [kv_l, kpe_l, kv_l]
    in_specs += [
        pl.BlockSpec((n_ctx, LANES), lambda b, h, i: (b, h)),
        pl.BlockSpec((n_ctx, LANES), lambda b, h, i: (b, 0)),
        pl.BlockSpec((n_ctx, LANES), lambda b, h, i: (b, N_HEADS + h)),
    ]
    args += [kv_c, kpe_c, kv_c]
    n_keys = n_lat + n_ctx
    tk = min(512, n_keys)
    return pl.pallas_call(
        functools.partial(_attn_kernel, n_lat=n_lat, n_ctx=n_ctx, tk=tk),
        grid=(n_batch, N_HEADS, nq),
        in_specs=in_specs,
        out_specs=pl.BlockSpec((tq, V_DIM), lambda b, h, i: (b * nq + i, h)),
        out_shape=jax.ShapeDtypeStruct((m, N_HEADS * V_DIM), BF16),
        scratch_shapes=[
            pltpu.VMEM((n_keys, HEAD_W), BF16),
            pltpu.VMEM((n_keys, V_DIM), BF16),
            pltpu.VMEM((tq, 1), F32),
            pltpu.VMEM((tq, 1), F32),
            pltpu.VMEM((tq, V_DIM), F32),
        ],
        compiler_params=_cparams(3),
        name="mla_attention",
    )(*args)


def _chan_dft_kernel(u_ref, cs_ref, o_ref):
    cs = cs_ref[...]
    for grp in range(F_GROUPS):
        c0 = grp * F_GROUP_DIM
        r = _dot(u_ref[:, c0:c0 + F_GROUP_DIM], cs)
        o_ref[0, :, c0:c0 + F_GROUP_DIM] = r[:, :F_GROUP_DIM].astype(o_ref.dtype)
        o_ref[1, :, c0:c0 + F_GROUP_DIM] = r[:, F_GROUP_DIM:].astype(o_ref.dtype)


def _mm_kernel(a_ref, b_ref, o_ref):
    o_ref[...] = _dot(a_ref[...], b_ref[...]).astype(o_ref.dtype)


def _fourier(p, cs, dft, *, n_batch, f_off):
    m = p.shape[0]
    n = m // n_batch
    fw = F_GROUPS * F_GROUP_DIM
    tm = min(512, n)
    nt = n // tm
    z = pl.pallas_call(
        _chan_dft_kernel,
        grid=(n_batch, nt),
        in_specs=[
            pl.BlockSpec((tm, fw), lambda b, t: (b * nt + t, _blk(f_off, fw))),
            pl.BlockSpec((F_GROUP_DIM, 2 * F_GROUP_DIM), lambda b, t: (0, 0)),
        ],
        out_specs=pl.BlockSpec((None, 2, tm, fw), lambda b, t: (b, 0, t, 0)),
        out_shape=jax.ShapeDtypeStruct((n_batch, 2, n, fw), BF16),
        compiler_params=_cparams(2),
        name="fourier_channel_dft",
    )(p, cs)
    z = z.reshape(n_batch, 2 * n, fw)
    tn = 512
    return pl.pallas_call(
        _mm_kernel,
        grid=(nt, n_batch, fw // tn),
        in_specs=[
            pl.BlockSpec((tm, 2 * n), lambda i, b, j: (i, 0)),
            pl.BlockSpec((None, 2 * n, tn), lambda i, b, j: (b, 0, j)),
        ],
        out_specs=pl.BlockSpec((tm, tn), lambda i, b, j: (b * nt + i, j)),
        out_shape=jax.ShapeDtypeStruct((m, fw), BF16),
        compiler_params=_cparams(3),
        name="fourier_position_dft",
    )(dft, z)


CONV_ROWS = 64


def _conv_kernel(a_ref, g_ref, ap_ref, gp_ref, an_ref, gn_ref, wb_ref, cb_ref, lg_ref, lb_ref,
                 o_ref, buf, shifted, conv_scr, *, ts):
    t = pl.program_id(1)
    last = pl.num_programs(1) - 1

    def glu(a, g):
        return a.astype(F32) * jax.nn.sigmoid(g.astype(F32))

    zeros = jnp.zeros((HALO, buf.shape[1]), F32)
    buf[0:HALO, :] = jnp.where(t > 0, glu(ap_ref[...], gp_ref[...]), zeros)
    buf[HALO:HALO + ts, :] = glu(a_ref[...], g_ref[...])
    buf[HALO + ts:2 * HALO + ts, :] = jnp.where(t < last, glu(an_ref[...], gn_ref[...]), zeros)

    span = ts + 3 * SUBLANES
    for s in range(1, SUBLANES):
        shifted[s - 1, 0:span, :] = buf[s:s + span, :]

    n_lane_chunks = buf.shape[1] // LANES
    groups = CONV_ROWS // SUBLANES

    def body(rc, carry):
        r0 = pl.multiple_of(rc * CONV_ROWS, CONV_ROWS)
        for lc in range(n_lane_chunks):
            ls = slice(lc * LANES, (lc + 1) * LANES)
            accs = [jnp.zeros((SUBLANES, LANES), F32) for _ in range(groups)]
            for k in range(CONV_K):
                whole, part = divmod(k + HALO - CONV_K // 2, SUBLANES)
                w = wb_ref[k * SUBLANES:(k + 1) * SUBLANES, ls]
                for j in range(groups):
                    row = r0 + (whole + j) * SUBLANES
                    if part == 0:
                        x = buf[pl.ds(row, SUBLANES), ls]
                    else:
                        x = shifted[part - 1, pl.ds(row, SUBLANES), ls]
                    accs[j] = accs[j] + w * x
            bias = cb_ref[:, ls]
            for j in range(groups):
                conv_scr[pl.ds(r0 + j * SUBLANES, SUBLANES), ls] = accs[j] + bias
        return carry

    lax.fori_loop(0, ts // CONV_ROWS, body, 0)

    v = conv_scr[...]
    mu = jnp.mean(v, axis=-1, keepdims=True)
    dv = v - mu
    var = jnp.mean(dv * dv, axis=-1, keepdims=True)
    y = dv * lax.rsqrt(var + EPS) * lg_ref[...] + lb_ref[...]
    o_ref[...] = _silu(y).astype(o_ref.dtype)


def _conv(p, wb, cb, lg, lb, *, n_batch, a_off, g_off):
    m = p.shape[0]
    cw = cb.shape[0]
    n = m // n_batch
    ts = min(256, n)
    nt = n // ts
    halo_per_tile = ts // HALO
    last_halo = m // HALO - 1

    def main(off):
        return pl.BlockSpec((ts, cw), lambda b, t: (b * nt + t, _blk(off, cw)))

    def prev(off):
        return pl.BlockSpec(
            (HALO, cw), lambda b, t: (jnp.maximum((b * nt + t) * halo_per_tile - 1, 0), _blk(off, cw)))

    def nxt(off):
        return pl.BlockSpec(
            (HALO, cw),
            lambda b, t: (jnp.minimum((b * nt + t + 1) * halo_per_tile, last_halo), _blk(off, cw)))

    vec = pl.BlockSpec((1, cw), lambda b, t: (0, 0))
    return pl.pallas_call(
        functools.partial(_conv_kernel, ts=ts),
        grid=(n_batch, nt),
        in_specs=[
            main(a_off), main(g_off), prev(a_off), prev(g_off), nxt(a_off), nxt(g_off),
            pl.BlockSpec((CONV_K * SUBLANES, cw), lambda b, t: (0, 0)),
            vec, vec, vec,
        ],
        out_specs=pl.BlockSpec((ts, cw), lambda b, t: (b * nt + t, 0)),
        out_shape=jax.ShapeDtypeStruct((m, cw), BF16),
        scratch_shapes=[
            pltpu.VMEM((ts + 2 * HALO, cw), F32),
            pltpu.VMEM((SUBLANES - 1, ts + 3 * SUBLANES, cw), F32),
            pltpu.VMEM((ts, cw), F32),
        ],
        compiler_params=_cparams(2),
        name="conformer_conv",
    )(p, p, p, p, p, p, wb, cb.reshape(1, cw), lg.reshape(1, cw), lb.reshape(1, cw))


def _merge_kernel(yf_ref, cv_ref, oa_ref, gf_ref, gc_ref, ga_ref, bf_ref, bc_ref, ba_ref,
                  wf_ref, wc_ref, wa_ref, o_ref):
    def gate(g_ref, b_ref):
        return jax.nn.sigmoid(g_ref[...].astype(F32) + b_ref[...])

    acc = gate(gf_ref, bf_ref) * _dot(yf_ref[...], wf_ref[...])
    acc = acc + gate(gc_ref, bc_ref) * _dot(cv_ref[...], wc_ref[...])
    acc = acc + gate(ga_ref, ba_ref) * _dot(oa_ref[...], wa_ref[...])
    o_ref[...] = acc.astype(o_ref.dtype)


def _merge(yf, cv, oa, p, b_gate, wf, wc, wa, *, gate_off):
    m = yf.shape[0]
    d = wf.shape[1]
    tm = min(1024, m)
    tn = 512
    nj = d // tn
    g0 = _blk(gate_off, tn)

    def act(a):
        return pl.BlockSpec((tm, a.shape[1]), lambda i, j: (i, 0))

    def gate(br):
        return pl.BlockSpec((tm, tn), lambda i, j: (i, g0 + br * nj + j))

    def bias(br):
        return pl.BlockSpec((1, tn), lambda i, j: (0, br * nj + j))

    def weight(w):
        return pl.BlockSpec((w.shape[0], tn), lambda i, j: (0, j))

    return pl.pallas_call(
        _merge_kernel,
        grid=(m // tm, nj),
        in_specs=[act(yf), act(cv), act(oa), gate(0), gate(1), gate(2),
                  bias(0), bias(1), bias(2), weight(wf), weight(wc), weight(wa)],
        out_specs=pl.BlockSpec((tm, tn), lambda i, j: (i, j)),
        out_shape=jax.ShapeDtypeStruct((m, d), BF16),
        compiler_params=_cparams(2),
        name="branch_merge",
    )(yf, cv, oa, p, p, p, b_gate, b_gate, b_gate, wf, wc, wa)


def _final_norm_kernel(x_ref, g_ref, o_ref):
    o_ref[...] = _rms(x_ref[...], g_ref[...])


def _final_norm(x, g):
    m, d = x.shape
    tm = min(512, m)
    return pl.pallas_call(
        _final_norm_kernel,
        grid=(m // tm,),
        in_specs=[pl.BlockSpec((tm, d), lambda i: (i, 0)), pl.BlockSpec((1, d), lambda i: (0, 0))],
        out_specs=pl.BlockSpec((tm, d), lambda i: (i, 0)),
        out_shape=jax.ShapeDtypeStruct((m, d), F32),
        compiler_params=_cparams(1),
        name="final_rmsnorm",
    )(x, g.reshape(1, d))


def _rope_tables(n_tok):
    rows = n_tok // GRID_W
    row = jnp.broadcast_to(jnp.arange(rows, dtype=F32)[:, None], (rows, GRID_W)).reshape(-1)
    col = jnp.broadcast_to(jnp.arange(GRID_W, dtype=F32)[None, :], (rows, GRID_W)).reshape(-1)
    n_freq = QK_ROPE // 4
    inv_freq = ROPE_BASE ** (-jnp.arange(n_freq, dtype=F32) / n_freq)
    ang = jnp.concatenate([row[:, None] * inv_freq, col[:, None] * inv_freq], axis=-1)
    pad = jnp.zeros((n_tok, LANES - QK_ROPE), F32)
    cos = jnp.concatenate([jnp.cos(ang), jnp.cos(ang), pad], axis=-1)
    sin = jnp.concatenate([jnp.sin(ang), jnp.sin(ang), pad], axis=-1)
    return cos, sin


def _dft_cos_sin(n):
    idx = jnp.arange(n, dtype=jnp.int32)
    ang = ((idx[:, None] * idx[None, :]) % n).astype(F32) * (2.0 * jnp.pi / n)
    scale = n ** -0.5
    return jnp.cos(ang) * scale, jnp.sin(ang) * scale


def kernel(x, c, ctx, c_ctx, ada_w, ada_b, norm_mix_g, w_in, b_gate, w_fourier, conv_w, conv_b,
           conv_ln_g, conv_ln_b, w_conv_out, q_norm_g, w_uq, kv_norm_g, w_ukv, w_mla_o, w_out,
           norm_ffn_g, w_ffn_gate, w_ffn_up, w_ffn_down, final_norm_g):
    n_batch, seq, d = x.shape
    n_ctx = ctx.shape[1]
    depth = ada_w.shape[0]
    assert n_batch < SUBLANES
    q_lora = q_norm_g.shape[1]
    kv_lora = kv_norm_g.shape[1]
    f_width = w_fourier.shape[1]
    c_width = conv_w.shape[2]
    assert (f_width, c_width, q_lora, kv_lora) == (PX_A - PX_F, PX_G - PX_A, PX_GATE - PX_QD,
                                                    PX_KPE - PX_KVD)
    f_end = f_width
    c_end = f_end + 2 * c_width
    q_end = c_end + q_lora
    kv_end = q_end + kv_lora
    kpe_end = kv_end + QK_ROPE

    zeros = lambda k: jnp.zeros((depth, d, k), w_in.dtype)
    w_in_r = jnp.concatenate([
        w_in[:, :, :c_end],
        w_in[:, :, q_end:kv_end],
        w_in[:, :, kv_end:kpe_end], zeros(LANES - QK_ROPE),
        zeros(PX_QD - PX_KPE - LANES),
        w_in[:, :, c_end:q_end],
        w_in[:, :, kpe_end:],
    ], axis=-1).astype(BF16)
    assert w_in_r.shape[-1] == PX_COLS
    w_uq_h = w_uq.reshape(depth, q_lora, N_HEADS, QK_NOPE + QK_ROPE)
    w_uq_r = jnp.concatenate(
        [w_uq_h, jnp.zeros((depth, q_lora, N_HEADS, HEAD_W - QK_NOPE - QK_ROPE), w_uq.dtype)],
        axis=-1).reshape(depth, q_lora, N_HEADS * HEAD_W).astype(BF16)
    w_ukv_r = w_ukv.reshape(depth, kv_lora, N_HEADS, 2, QK_NOPE).transpose(0, 1, 3, 2, 4).reshape(
        depth, kv_lora, 2 * N_HEADS * QK_NOPE).astype(BF16)
    w_fourier_b = w_fourier.astype(BF16)
    w_conv_out_b = w_conv_out.astype(BF16)
    w_mla_o_b = w_mla_o.astype(BF16)
    w_out_b = w_out.astype(BF16)
    w_gate_b = w_ffn_gate.astype(BF16)
    w_up_b = w_ffn_up.astype(BF16)
    w_down_b = w_ffn_down.astype(BF16)
    conv_wb = jnp.repeat(conv_w, SUBLANES, axis=1)
    b_gate3 = b_gate.reshape(depth, 1, -1)

    cos_l, sin_l = _rope_tables(seq)
    cos_l = jnp.tile(cos_l, (n_batch, 1))
    sin_l = jnp.tile(sin_l, (n_batch, 1))
    cos_c = jnp.ones((n_batch * n_ctx, LANES), F32)
    sin_c = jnp.zeros((n_batch * n_ctx, LANES), F32)
    c128, s128 = _dft_cos_sin(F_GROUP_DIM)
    cs = jnp.concatenate([c128, s128], axis=1).astype(BF16)
    cn, sn = _dft_cos_sin(seq)
    dft_l = jnp.concatenate([cn, -sn], axis=1).astype(BF16)
    cc, sc = _dft_cos_sin(n_ctx)
    dft_c = jnp.concatenate([cc, -sc], axis=1).astype(BF16)

    c8 = jnp.concatenate(
        [c, c_ctx[None, :], jnp.zeros((SUBLANES - n_batch - 1, d), c.dtype)], axis=0)
    mod_all = _ada(c8, ada_w, ada_b).reshape(depth, SUBLANES, 1, 6 * d)

    lat = _Rows(n_batch * seq, seq, None)
    cxt = _Rows(n_batch * n_ctx, None, n_batch)
    xs = x.reshape(n_batch * seq, d)
    cs_rows = ctx.reshape(n_batch * n_ctx, d)

    def mix_and_ffn(i, rows, h_in, p, o_attn, nb):
        mod3 = mod_all[i]
        yf = _fourier(p, cs, dft_l if rows is lat else dft_c, n_batch=nb, f_off=PX_F)
        cv = _conv(p, conv_wb[i], conv_b[i], conv_ln_g[i], conv_ln_b[i],
                   n_batch=nb, a_off=PX_A, g_off=PX_G)
        merged = _merge(yf, cv, o_attn, p, b_gate3[i], w_fourier_b[i], w_conv_out_b[i],
                        w_mla_o_b[i], gate_off=PX_GATE)
        mixed = _resmm(merged, w_out_b[i], h_in, mod3, rows, gate_chunk=2, tm=1024, tn=512)
        hid = _ffn_up(mixed, norm_ffn_g[i], mod3, w_gate_b[i], w_up_b[i], rows)
        return _resmm(hid, w_down_b[i], mixed, mod3, rows, gate_chunk=5, tm=512, tn=512)

    for i in range(depth):
        last = i == depth - 1
        mod3 = mod_all[i]
        px = _inproj(xs, norm_mix_g[i], mod3, w_in_r[i], lat, shift_chunk=0, tn=768)
        if last:
            pc = _inproj(cs_rows, norm_mix_g[i], mod3, w_in_r[i][:, PX_KVD:PX_QD], cxt,
                         shift_chunk=0, tn=PKV_COLS)
            kvd_off, kpe_off = PKV_KVD, PKV_KPE
        else:
            pc = _inproj(cs_rows, norm_mix_g[i], mod3, w_in_r[i], cxt, shift_chunk=0, tn=768)
            kvd_off, kpe_off = PX_KVD, PX_KPE

        q_x = _qup(px, q_norm_g[i], cos_l, sin_l, w_uq_r[i], qd_off=PX_QD)
        kv_x, kpe_x = _kvup(px, kv_norm_g[i], cos_l, sin_l, w_ukv_r[i],
                            kvd_off=PX_KVD, kpe_off=PX_KPE)
        kv_c, kpe_c = _kvup(pc, kv_norm_g[i], cos_c, sin_c, w_ukv_r[i],
                            kvd_off=kvd_off, kpe_off=kpe_off)
        o_x = _attention(q_x, kv_c, kpe_c, n_batch=n_batch, n_ctx=n_ctx,
                         q_rows_per_batch=seq, lat=(kv_x, kpe_x))
        if not last:
            q_c = _qup(pc, q_norm_g[i], cos_c, sin_c, w_uq_r[i], qd_off=PX_QD)
            o_c = _attention(q_c, kv_c, kpe_c, n_batch=n_batch, n_ctx=n_ctx,
                             q_rows_per_batch=n_ctx)
            cs_rows = mix_and_ffn(i, cxt, cs_rows, pc, o_c, n_batch)
        xs = mix_and_ffn(i, lat, xs, px, o_x, n_batch)

    return _final_norm(xs, final_norm_g).reshape(n_batch, seq, d)
```

```python
import functools

import jax
import jax.numpy as jnp
from jax import lax
from jax.experimental import pallas as pl
from jax.experimental.pallas import tpu as pltpu

F32 = jnp.float32
BF16 = jnp.bfloat16

GRID_W = 64
N_HEADS = 16
QK_NOPE = 128
QK_ROPE = 64
V_DIM = 128
F_GROUPS = 8
F_GROUP_DIM = 128
CONV_K = 31
ROPE_BASE = 10000.0
ATTN_SCALE = (QK_NOPE + QK_ROPE) ** -0.5
Q_SCALE = ATTN_SCALE * 1.4426950408889634
EPS = 1e-6

LANES = 128
SUBLANES = 8
BF16_ROWS = 16
VMEM_LIMIT = 56 * 1024 * 1024

HEAD_W = 2 * LANES
HALO = BF16_ROWS

PX_F = 0
PX_A = 1024
PX_G = 2048
PX_KVD = 3072
PX_KPE = 3584
PX_QD = 3840
PX_GATE = 4608
PX_COLS = 10752
PKV_KVD = 0
PKV_KPE = 512
PKV_COLS = 768


def _cparams(n_axes):
    return pltpu.CompilerParams(
        dimension_semantics=("arbitrary",) * n_axes, vmem_limit_bytes=VMEM_LIMIT)


def _blk(off, width):
    assert off % width == 0, (off, width)
    return off // width


def _dot(a, b):
    return jnp.dot(a, b, preferred_element_type=F32)


def _silu(v):
    return v * jax.nn.sigmoid(v)


def _rms(x, g):
    r = lax.rsqrt(jnp.mean(x * x, axis=-1, keepdims=True) + EPS)
    return (x * r) * g


def _rope128(pe, cos, sin):
    lane = lax.broadcasted_iota(jnp.int32, pe.shape, 1)
    half = QK_ROPE // 2
    rot = jnp.where(lane < half, -pltpu.roll(pe, LANES - half, 1), pltpu.roll(pe, half, 1))
    return pe * cos + rot * sin


def _ada_kernel(c_ref, w_ref, b_ref, o_ref):
    a = _silu(c_ref[...]).astype(BF16)
    o_ref[...] = _dot(a, w_ref[...].astype(BF16)) + b_ref[...]


def _ada(c8, ada_w, ada_b):
    depth, d, n6 = ada_w.shape
    tn = 1024
    return pl.pallas_call(
        _ada_kernel,
        grid=(depth, n6 // tn),
        in_specs=[
            pl.BlockSpec((SUBLANES, d), lambda l, j: (0, 0)),
            pl.BlockSpec((None, d, tn), lambda l, j: (l, 0, j)),
            pl.BlockSpec((None, 1, tn), lambda l, j: (l, 0, j)),
        ],
        out_specs=pl.BlockSpec((None, SUBLANES, tn), lambda l, j: (l, 0, j)),
        out_shape=jax.ShapeDtypeStruct((depth, SUBLANES, n6), F32),
        compiler_params=_cparams(2),
        name="ada_mod",
    )(c8, ada_w, ada_b.reshape(depth, 1, n6))


class _Rows:
    def __init__(self, m, rows_per_batch, fixed_mod_row):
        self.m = m
        self.rows_per_batch = rows_per_batch
        self.fixed_mod_row = fixed_mod_row

    def tile(self, preferred):
        return min(preferred, self.rows_per_batch or self.m)

    def mod_row(self, tm):
        if self.fixed_mod_row is not None:
            row = self.fixed_mod_row
            return lambda i: row
        rpb = self.rows_per_batch
        assert rpb % tm == 0
        return lambda i: (i * tm) // rpb


def _mod_spec(rows, tm, d, chunk):
    row = rows.mod_row(tm)
    return pl.BlockSpec((None, 1, d), lambda i, j: (row(i), 0, chunk))


def _inproj_kernel(x_ref, g_ref, sc_ref, sh_ref, w_ref, o_ref, h_ref):
    @pl.when(pl.program_id(1) == 0)
    def _():
        h = _rms(x_ref[...], g_ref[...]) * (1.0 + sc_ref[...]) + sh_ref[...]
        h_ref[...] = h.astype(BF16)

    o_ref[...] = _dot(h_ref[...], w_ref[...]).astype(o_ref.dtype)


def _wspec(w, layer, tn, col0=0):
    return pl.BlockSpec((None, w.shape[1], tn), lambda i, j: (layer, 0, col0 + j))


def _inproj(x, g, mod3, w, layer, rows, *, shift_chunk, tn, col_off=0, n=None):
    m, d = x.shape
    n = n or w.shape[2]
    tm = rows.tile(1024)
    return pl.pallas_call(
        _inproj_kernel,
        grid=(m // tm, n // tn),
        in_specs=[
            pl.BlockSpec((tm, d), lambda i, j: (i, 0)),
            pl.BlockSpec((1, d), lambda i, j: (0, 0)),
            _mod_spec(rows, tm, d, shift_chunk + 1),
            _mod_spec(rows, tm, d, shift_chunk),
            _wspec(w, layer, tn, _blk(col_off, tn)),
        ],
        out_specs=pl.BlockSpec((tm, tn), lambda i, j: (i, j)),
        out_shape=jax.ShapeDtypeStruct((m, n), BF16),
        scratch_shapes=[pltpu.VMEM((tm, d), BF16)],
        compiler_params=_cparams(2),
        name="norm_mod_proj",
    )(x, g.reshape(1, d), mod3, mod3, w)


def _ffn_up_kernel(x_ref, g_ref, sc_ref, sh_ref, wg_ref, wu_ref, o_ref, h_ref):
    @pl.when(pl.program_id(1) == 0)
    def _():
        h = _rms(x_ref[...], g_ref[...]) * (1.0 + sc_ref[...]) + sh_ref[...]
        h_ref[...] = h.astype(BF16)

    hb = h_ref[...]
    o_ref[...] = (_silu(_dot(hb, wg_ref[...])) * _dot(hb, wu_ref[...])).astype(o_ref.dtype)


def _ffn_up(x, g, mod3, wg, wu, layer, rows):
    m, d = x.shape
    n = wg.shape[2]
    tm = rows.tile(1024)
    tn = 512
    return pl.pallas_call(
        _ffn_up_kernel,
        grid=(m // tm, n // tn),
        in_specs=[
            pl.BlockSpec((tm, d), lambda i, j: (i, 0)),
            pl.BlockSpec((1, d), lambda i, j: (0, 0)),
            _mod_spec(rows, tm, d, 4),
            _mod_spec(rows, tm, d, 3),
            _wspec(wg, layer, tn),
            _wspec(wu, layer, tn),
        ],
        out_specs=pl.BlockSpec((tm, tn), lambda i, j: (i, j)),
        out_shape=jax.ShapeDtypeStruct((m, n), BF16),
        scratch_shapes=[pltpu.VMEM((tm, d), BF16)],
        compiler_params=_cparams(2),
        name="ffn_gate_up",
    )(x, g.reshape(1, d), mod3, mod3, wg, wu)


def _resmm_kernel(a_ref, w_ref, res_ref, gate_ref, o_ref):
    o_ref[...] = res_ref[...] + gate_ref[...] * _dot(a_ref[...], w_ref[...])


def _resmm(a, w, layer, res, mod3, rows, *, gate_chunk, tm, tn):
    m, k = a.shape
    d = w.shape[2]
    tm = rows.tile(tm)
    row = rows.mod_row(tm)
    per_chunk = d // tn
    return pl.pallas_call(
        _resmm_kernel,
        grid=(m // tm, d // tn),
        in_specs=[
            pl.BlockSpec((tm, k), lambda i, j: (i, 0)),
            _wspec(w, layer, tn),
            pl.BlockSpec((tm, tn), lambda i, j: (i, j)),
            pl.BlockSpec((None, 1, tn), lambda i, j: (row(i), 0, gate_chunk * per_chunk + j)),
        ],
        out_specs=pl.BlockSpec((tm, tn), lambda i, j: (i, j)),
        out_shape=jax.ShapeDtypeStruct((m, d), F32),
        compiler_params=_cparams(2),
        name="proj_gated_residual",
    )(a, w, res, mod3)


def _qup_kernel(qd_ref, g_ref, cos_ref, sin_ref, w_ref, o_ref, h_ref, *, heads_per_tile):
    @pl.when(pl.program_id(1) == 0)
    def _():
        h_ref[...] = _rms(qd_ref[...].astype(F32), g_ref[...]).astype(BF16)

    acc = _dot(h_ref[...], w_ref[...]) * Q_SCALE
    cos = cos_ref[...]
    sin = sin_ref[...]
    for hh in range(heads_per_tile):
        c0 = hh * HEAD_W
        o_ref[:, c0:c0 + LANES] = acc[:, c0:c0 + LANES].astype(o_ref.dtype)
        pe = acc[:, c0 + LANES:c0 + HEAD_W]
        o_ref[:, c0 + LANES:c0 + HEAD_W] = _rope128(pe, cos, sin).astype(o_ref.dtype)


def _qup(p, g, cos, sin, w, layer, *, qd_off):
    m = p.shape[0]
    _, kq, n = w.shape
    tm = min(1024, m)
    tn = 1024
    return pl.pallas_call(
        functools.partial(_qup_kernel, heads_per_tile=tn // HEAD_W),
        grid=(m // tm, n // tn),
        in_specs=[
            pl.BlockSpec((tm, kq), lambda i, j: (i, _blk(qd_off, kq))),
            pl.BlockSpec((1, kq), lambda i, j: (0, 0)),
            pl.BlockSpec((tm, LANES), lambda i, j: (i, 0)),
            pl.BlockSpec((tm, LANES), lambda i, j: (i, 0)),
            _wspec(w, layer, tn),
        ],
        out_specs=pl.BlockSpec((tm, tn), lambda i, j: (i, j)),
        out_shape=jax.ShapeDtypeStruct((m, n), BF16),
        scratch_shapes=[pltpu.VMEM((tm, kq), BF16)],
        compiler_params=_cparams(2),
        name="mla_q_up",
    )(p, g.reshape(1, kq), cos, sin, w)


def _kvup_kernel(kvd_ref, g_ref, kpe_ref, cos_ref, sin_ref, w_ref, o_ref, okpe_ref, h_ref):
    @pl.when(pl.program_id(1) == 0)
    def _():
        h_ref[...] = _rms(kvd_ref[...].astype(F32), g_ref[...]).astype(BF16)
        pe = kpe_ref[...].astype(F32)
        okpe_ref[...] = _rope128(pe, cos_ref[...], sin_ref[...]).astype(okpe_ref.dtype)

    o_ref[...] = _dot(h_ref[...], w_ref[...]).astype(o_ref.dtype)


def _kvup(p, g, cos, sin, w, layer, *, kvd_off, kpe_off):
    m = p.shape[0]
    _, kk, n = w.shape
    tm = min(1024, m)
    tn = 1024
    return pl.pallas_call(
        _kvup_kernel,
        grid=(m // tm, n // tn),
        in_specs=[
            pl.BlockSpec((tm, kk), lambda i, j: (i, _blk(kvd_off, kk))),
            pl.BlockSpec((1, kk), lambda i, j: (0, 0)),
            pl.BlockSpec((tm, LANES), lambda i, j: (i, _blk(kpe_off, LANES))),
            pl.BlockSpec((tm, LANES), lambda i, j: (i, 0)),
            pl.BlockSpec((tm, LANES), lambda i, j: (i, 0)),
            _wspec(w, layer, tn),
        ],
        out_specs=[
            pl.BlockSpec((tm, tn), lambda i, j: (i, j)),
            pl.BlockSpec((tm, LANES), lambda i, j: (i, 0)),
        ],
        out_shape=[
            jax.ShapeDtypeStruct((m, n), BF16),
            jax.ShapeDtypeStruct((m, LANES), BF16),
        ],
        scratch_shapes=[pltpu.VMEM((tm, kk), BF16)],
        compiler_params=_cparams(2),
        name="mla_kv_up",
    )(p, g.reshape(1, kk), p, cos, sin, w)


ATTN_TK = HEAD_W
ATTN_SUB = 256
ATTN_TQ = 1024


def _attn_kernel(*refs, n_lat, n_ctx):
    if n_lat:
        q_ref, kl_ref, kpl_ref, vl_ref, kc_ref, kpc_ref, vc_ref, o_ref, kt_scr, s_scr = refs
        segments = ((kl_ref, kpl_ref, vl_ref, n_lat), (kc_ref, kpc_ref, vc_ref, n_ctx))
    else:
        q_ref, kc_ref, kpc_ref, vc_ref, o_ref, kt_scr, s_scr = refs
        segments = ((kc_ref, kpc_ref, vc_ref, n_ctx),)
    tk = ATTN_TK
    chunks = []
    base = 0
    for k_ref, kp_ref, v_ref, n in segments:
        assert n % tk == 0
        chunks += [(base + r, k_ref, kp_ref, v_ref, r) for r in range(0, n, tk)]
        base += n

    @pl.when(pl.program_id(2) == 0)
    def _():
        for off, k_ref, kp_ref, _, r in chunks:
            kt_scr[0:LANES, off:off + tk] = k_ref[r:r + tk, :].T
            kt_scr[LANES:HEAD_W, off:off + tk] = kp_ref[r:r + tk, :].T

    tq = q_ref.shape[0]
    sub = min(ATTN_SUB, tq)
    for r0 in range(0, tq, sub):
        rows = slice(r0, r0 + sub)
        q = q_ref[rows, :]
        m128 = None
        for off, *_ in chunks:
            s = _dot(q, kt_scr[:, off:off + tk])
            s_scr[rows, off:off + tk] = s
            mx = jnp.maximum(s[:, :LANES], s[:, LANES:])
            m128 = mx if m128 is None else jnp.maximum(m128, mx)
        mb = jnp.broadcast_to(jnp.max(m128, axis=-1, keepdims=True), (sub, LANES))

        l128 = jnp.zeros((sub, LANES), F32)
        acc = jnp.zeros((sub, V_DIM), F32)
        for off, _, _, v_ref, r in chunks:
            p0 = jnp.exp2(s_scr[rows, off:off + LANES] - mb)
            p1 = jnp.exp2(s_scr[rows, off + LANES:off + tk] - mb)
            l128 = l128 + (p0 + p1)
            p = jnp.concatenate([p0, p1], axis=1).astype(BF16)
            acc = acc + _dot(p, v_ref[r:r + tk, :])
        l = jnp.sum(l128, axis=-1, keepdims=True)
        o_ref[rows, :] = (acc / l).astype(o_ref.dtype)


def _attention(q, kv_c, kpe_c, *, n_batch, n_ctx, q_rows_per_batch, lat=None):
    m = q.shape[0]
    nq_rows = q_rows_per_batch
    tq = min(ATTN_TQ, nq_rows)
    nq = nq_rows // tq
    n_lat = 0
    in_specs = [pl.BlockSpec((tq, HEAD_W), lambda b, h, i: (b * nq + i, h))]
    args = [q]
    if lat is not None:
        kv_l, kpe_l = lat
        n_lat = kv_l.shape[0] // n_batch
        in_specs += [
            pl.BlockSpec((n_lat, LANES), lambda b, h, i: (b, h)),
            pl.BlockSpec((n_lat, LANES), lambda b, h, i: (b, 0)),
            pl.BlockSpec((n_lat, LANES), lambda b, h, i: (b, N_HEADS + h)),
        ]
        args += [kv_l, kpe_l, kv_l]
    in_specs += [
        pl.BlockSpec((n_ctx, LANES), lambda b, h, i: (b, h)),
        pl.BlockSpec((n_ctx, LANES), lambda b, h, i: (b, 0)),
        pl.BlockSpec((n_ctx, LANES), lambda b, h, i: (b, N_HEADS + h)),
    ]
    args += [kv_c, kpe_c, kv_c]
    n_keys = n_lat + n_ctx
    return pl.pallas_call(
        functools.partial(_attn_kernel, n_lat=n_lat, n_ctx=n_ctx),
        grid=(n_batch, N_HEADS, nq),
        in_specs=in_specs,
        out_specs=pl.BlockSpec((tq, V_DIM), lambda b, h, i: (b * nq + i, h)),
        out_shape=jax.ShapeDtypeStruct((m, N_HEADS * V_DIM), BF16),
        scratch_shapes=[
            pltpu.VMEM((HEAD_W, n_keys), BF16),
            pltpu.VMEM((tq, n_keys), F32),
        ],
        compiler_params=_cparams(3),
        name="mla_attention",
    )(*args)


def _chan_dft_kernel(u_ref, cs_ref, o_ref):
    cs = cs_ref[...]
    for grp in range(F_GROUPS):
        c0 = grp * F_GROUP_DIM
        r = _dot(u_ref[:, c0:c0 + F_GROUP_DIM], cs)
        o_ref[0, :, c0:c0 + F_GROUP_DIM] = r[:, :F_GROUP_DIM].astype(o_ref.dtype)
        o_ref[1, :, c0:c0 + F_GROUP_DIM] = r[:, F_GROUP_DIM:].astype(o_ref.dtype)


def _mm_kernel(a_ref, b_ref, o_ref):
    o_ref[...] = _dot(a_ref[...], b_ref[...]).astype(o_ref.dtype)


def _fourier(p, cs, dft, *, n_batch, f_off):
    m = p.shape[0]
    n = m // n_batch
    fw = F_GROUPS * F_GROUP_DIM
    tm = min(512, n)
    nt = n // tm
    z = pl.pallas_call(
        _chan_dft_kernel,
        grid=(n_batch, nt),
        in_specs=[
            pl.BlockSpec((tm, fw), lambda b, t: (b * nt + t, _blk(f_off, fw))),
            pl.BlockSpec((F_GROUP_DIM, 2 * F_GROUP_DIM), lambda b, t: (0, 0)),
        ],
        out_specs=pl.BlockSpec((None, 2, tm, fw), lambda b, t: (b, 0, t, 0)),
        out_shape=jax.ShapeDtypeStruct((n_batch, 2, n, fw), BF16),
        compiler_params=_cparams(2),
        name="fourier_channel_dft",
    )(p, cs)
    z = z.reshape(n_batch, 2 * n, fw)
    tn = 512
    return pl.pallas_call(
        _mm_kernel,
        grid=(nt, n_batch, fw // tn),
        in_specs=[
            pl.BlockSpec((tm, 2 * n), lambda i, b, j: (i, 0)),
            pl.BlockSpec((None, 2 * n, tn), lambda i, b, j: (b, 0, j)),
        ],
        out_specs=pl.BlockSpec((tm, tn), lambda i, b, j: (b * nt + i, j)),
        out_shape=jax.ShapeDtypeStruct((m, fw), BF16),
        compiler_params=_cparams(3),
        name="fourier_position_dft",
    )(dft, z)


CONV_ROWS = 64


def _conv_kernel(a_ref, g_ref, ap_ref, gp_ref, an_ref, gn_ref, wb_ref, cb_ref, lg_ref, lb_ref,
                 o_ref, buf, shifted, conv_scr, *, ts):
    t = pl.program_id(1)
    last = pl.num_programs(1) - 1

    def glu(a, g):
        return a.astype(F32) * jax.nn.sigmoid(g.astype(F32))

    zeros = jnp.zeros((HALO, buf.shape[1]), F32)
    buf[0:HALO, :] = jnp.where(t > 0, glu(ap_ref[...], gp_ref[...]), zeros)
    buf[HALO:HALO + ts, :] = glu(a_ref[...], g_ref[...])
    buf[HALO + ts:2 * HALO + ts, :] = jnp.where(t < last, glu(an_ref[...], gn_ref[...]), zeros)

    span = ts + 3 * SUBLANES
    for s in range(1, SUBLANES):
        shifted[s - 1, 0:span, :] = buf[s:s + span, :]

    n_lane_chunks = buf.shape[1] // LANES
    groups = CONV_ROWS // SUBLANES

    def body(rc, carry):
        r0 = pl.multiple_of(rc * CONV_ROWS, CONV_ROWS)
        for lc in range(n_lane_chunks):
            ls = slice(lc * LANES, (lc + 1) * LANES)
            accs = [jnp.zeros((SUBLANES, LANES), F32) for _ in range(groups)]
            for k in range(CONV_K):
                whole, part = divmod(k + HALO - CONV_K // 2, SUBLANES)
                w = wb_ref[k * SUBLANES:(k + 1) * SUBLANES, ls]
                for j in range(groups):
                    row = r0 + (whole + j) * SUBLANES
                    if part == 0:
                        x = buf[pl.ds(row, SUBLANES), ls]
                    else:
                        x = shifted[part - 1, pl.ds(row, SUBLANES), ls]
                    accs[j] = accs[j] + w * x
            bias = cb_ref[:, ls]
            for j in range(groups):
                conv_scr[pl.ds(r0 + j * SUBLANES, SUBLANES), ls] = accs[j] + bias
        return carry

    lax.fori_loop(0, ts // CONV_ROWS, body, 0)

    v = conv_scr[...]
    mu = jnp.mean(v, axis=-1, keepdims=True)
    dv = v - mu
    var = jnp.mean(dv * dv, axis=-1, keepdims=True)
    y = dv * lax.rsqrt(var + EPS) * lg_ref[...] + lb_ref[...]
    o_ref[...] = _silu(y).astype(o_ref.dtype)


def _conv(p, wb, cb, lg, lb, *, n_batch, a_off, g_off):
    m = p.shape[0]
    cw = cb.shape[0]
    n = m // n_batch
    ts = min(256, n)
    nt = n // ts
    halo_per_tile = ts // HALO
    last_halo = m // HALO - 1

    def main(off):
        return pl.BlockSpec((ts, cw), lambda b, t: (b * nt + t, _blk(off, cw)))

    def prev(off):
        return pl.BlockSpec(
            (HALO, cw), lambda b, t: (jnp.maximum((b * nt + t) * halo_per_tile - 1, 0), _blk(off, cw)))

    def nxt(off):
        return pl.BlockSpec(
            (HALO, cw),
            lambda b, t: (jnp.minimum((b * nt + t + 1) * halo_per_tile, last_halo), _blk(off, cw)))

    vec = pl.BlockSpec((1, cw), lambda b, t: (0, 0))
    return pl.pallas_call(
        functools.partial(_conv_kernel, ts=ts),
        grid=(n_batch, nt),
        in_specs=[
            main(a_off), main(g_off), prev(a_off), prev(g_off), nxt(a_off), nxt(g_off),
            pl.BlockSpec((CONV_K * SUBLANES, cw), lambda b, t: (0, 0)),
            vec, vec, vec,
        ],
        out_specs=pl.BlockSpec((ts, cw), lambda b, t: (b * nt + t, 0)),
        out_shape=jax.ShapeDtypeStruct((m, cw), BF16),
        scratch_shapes=[
            pltpu.VMEM((ts + 2 * HALO, cw), F32),
            pltpu.VMEM((SUBLANES - 1, ts + 3 * SUBLANES, cw), F32),
            pltpu.VMEM((ts, cw), F32),
        ],
        compiler_params=_cparams(2),
        name="conformer_conv",
    )(p, p, p, p, p, p, wb, cb.reshape(1, cw), lg.reshape(1, cw), lb.reshape(1, cw))


def _merge_kernel(yf_ref, cv_ref, oa_ref, gf_ref, gc_ref, ga_ref, bf_ref, bc_ref, ba_ref,
                  wf_ref, wc_ref, wa_ref, o_ref):
    def gate(g_ref, b_ref):
        return jax.nn.sigmoid(g_ref[...].astype(F32) + b_ref[...])

    acc = gate(gf_ref, bf_ref) * _dot(yf_ref[...], wf_ref[...])
    acc = acc + gate(gc_ref, bc_ref) * _dot(cv_ref[...], wc_ref[...])
    acc = acc + gate(ga_ref, ba_ref) * _dot(oa_ref[...], wa_ref[...])
    o_ref[...] = acc.astype(o_ref.dtype)


def _merge(yf, cv, oa, p, b_gate, wf, wc, wa, layer, *, gate_off):
    m = yf.shape[0]
    d = wf.shape[2]
    tm = min(1024, m)
    tn = 512
    nj = d // tn
    g0 = _blk(gate_off, tn)

    def act(a):
        return pl.BlockSpec((tm, a.shape[1]), lambda i, j: (i, 0))

    def gate(br):
        return pl.BlockSpec((tm, tn), lambda i, j: (i, g0 + br * nj + j))

    def bias(br):
        return pl.BlockSpec((None, 1, tn), lambda i, j: (layer, 0, br * nj + j))

    def weight(w):
        return _wspec(w, layer, tn)

    return pl.pallas_call(
        _merge_kernel,
        grid=(m // tm, nj),
        in_specs=[act(yf), act(cv), act(oa), gate(0), gate(1), gate(2),
                  bias(0), bias(1), bias(2), weight(wf), weight(wc), weight(wa)],
        out_specs=pl.BlockSpec((tm, tn), lambda i, j: (i, j)),
        out_shape=jax.ShapeDtypeStruct((m, d), BF16),
        compiler_params=_cparams(2),
        name="branch_merge",
    )(yf, cv, oa, p, p, p, b_gate, b_gate, b_gate, wf, wc, wa)


def _final_norm_kernel(x_ref, g_ref, o_ref):
    o_ref[...] = _rms(x_ref[...], g_ref[...])


def _final_norm(x, g):
    m, d = x.shape
    tm = min(512, m)
    return pl.pallas_call(
        _final_norm_kernel,
        grid=(m // tm,),
        in_specs=[pl.BlockSpec((tm, d), lambda i: (i, 0)), pl.BlockSpec((1, d), lambda i: (0, 0))],
        out_specs=pl.BlockSpec((tm, d), lambda i: (i, 0)),
        out_shape=jax.ShapeDtypeStruct((m, d), F32),
        compiler_params=_cparams(1),
        name="final_rmsnorm",
    )(x, g.reshape(1, d))


def _rope_tables(n_tok):
    rows = n_tok // GRID_W
    row = jnp.broadcast_to(jnp.arange(rows, dtype=F32)[:, None], (rows, GRID_W)).reshape(-1)
    col = jnp.broadcast_to(jnp.arange(GRID_W, dtype=F32)[None, :], (rows, GRID_W)).reshape(-1)
    n_freq = QK_ROPE // 4
    inv_freq = ROPE_BASE ** (-jnp.arange(n_freq, dtype=F32) / n_freq)
    ang = jnp.concatenate([row[:, None] * inv_freq, col[:, None] * inv_freq], axis=-1)
    pad = jnp.zeros((n_tok, LANES - QK_ROPE), F32)
    cos = jnp.concatenate([jnp.cos(ang), jnp.cos(ang), pad], axis=-1)
    sin = jnp.concatenate([jnp.sin(ang), jnp.sin(ang), pad], axis=-1)
    return cos, sin


def _dft_cos_sin(n):
    n2 = min(n, GRID_W)
    n1 = n // n2
    k = jnp.arange(n, dtype=jnp.int32)[:, None]
    ang1 = (((k % n1) * jnp.arange(n1, dtype=jnp.int32)[None, :]) % n1).astype(F32) * (2.0 * jnp.pi / n1)
    ang2 = ((k * jnp.arange(n2, dtype=jnp.int32)[None, :]) % n).astype(F32) * (2.0 * jnp.pi / n)
    scale = n ** -0.5
    c1, s1 = jnp.cos(ang1)[:, :, None] * scale, jnp.sin(ang1)[:, :, None] * scale
    c2, s2 = jnp.cos(ang2)[:, None, :], jnp.sin(ang2)[:, None, :]
    return (c1 * c2 - s1 * s2).reshape(n, n), (s1 * c2 + c1 * s2).reshape(n, n)


def kernel(x, c, ctx, c_ctx, ada_w, ada_b, norm_mix_g, w_in, b_gate, w_fourier, conv_w, conv_b,
           conv_ln_g, conv_ln_b, w_conv_out, q_norm_g, w_uq, kv_norm_g, w_ukv, w_mla_o, w_out,
           norm_ffn_g, w_ffn_gate, w_ffn_up, w_ffn_down, final_norm_g):
    n_batch, seq, d = x.shape
    n_ctx = ctx.shape[1]
    depth = ada_w.shape[0]
    assert n_batch < SUBLANES
    q_lora = q_norm_g.shape[1]
    kv_lora = kv_norm_g.shape[1]
    f_width = w_fourier.shape[1]
    c_width = conv_w.shape[2]
    assert (f_width, c_width, q_lora, kv_lora) == (PX_A - PX_F, PX_G - PX_A, PX_GATE - PX_QD,
                                                    PX_KPE - PX_KVD)
    f_end = f_width
    c_end = f_end + 2 * c_width
    q_end = c_end + q_lora
    kv_end = q_end + kv_lora
    kpe_end = kv_end + QK_ROPE

    zeros = lambda k: jnp.zeros((depth, d, k), w_in.dtype)
    w_in_r = jnp.concatenate([
        w_in[:, :, :c_end],
        w_in[:, :, q_end:kv_end],
        w_in[:, :, kv_end:kpe_end], zeros(LANES - QK_ROPE),
        zeros(PX_QD - PX_KPE - LANES),
        w_in[:, :, c_end:q_end],
        w_in[:, :, kpe_end:],
    ], axis=-1).astype(BF16)
    assert w_in_r.shape[-1] == PX_COLS
    w_uq_h = w_uq.reshape(depth, q_lora, N_HEADS, QK_NOPE + QK_ROPE)
    w_uq_r = jnp.concatenate(
        [w_uq_h, jnp.zeros((depth, q_lora, N_HEADS, HEAD_W - QK_NOPE - QK_ROPE), w_uq.dtype)],
        axis=-1).reshape(depth, q_lora, N_HEADS * HEAD_W).astype(BF16)
    w_ukv_r = w_ukv.reshape(depth, kv_lora, N_HEADS, 2, QK_NOPE).transpose(0, 1, 3, 2, 4).reshape(
        depth, kv_lora, 2 * N_HEADS * QK_NOPE).astype(BF16)
    w_fourier_b = w_fourier.astype(BF16)
    w_conv_out_b = w_conv_out.astype(BF16)
    w_mla_o_b = w_mla_o.astype(BF16)
    w_out_b = w_out.astype(BF16)
    w_gate_b = w_ffn_gate.astype(BF16)
    w_up_b = w_ffn_up.astype(BF16)
    w_down_b = w_ffn_down.astype(BF16)
    conv_wb = jnp.repeat(conv_w, SUBLANES, axis=1)
    b_gate3 = b_gate.reshape(depth, 1, -1)

    cos_l, sin_l = _rope_tables(seq)
    cos_l = jnp.tile(cos_l, (n_batch, 1))
    sin_l = jnp.tile(sin_l, (n_batch, 1))
    cos_c = jnp.ones((n_batch * n_ctx, LANES), F32)
    sin_c = jnp.zeros((n_batch * n_ctx, LANES), F32)
    c128, s128 = _dft_cos_sin(F_GROUP_DIM)
    cs = jnp.concatenate([c128, s128], axis=1).astype(BF16)
    cn, sn = _dft_cos_sin(seq)
    dft_l = jnp.concatenate([cn, -sn], axis=1).astype(BF16)
    cc, sc = _dft_cos_sin(n_ctx)
    dft_c = jnp.concatenate([cc, -sc], axis=1).astype(BF16)

    c8 = jnp.concatenate(
        [c, c_ctx[None, :], jnp.zeros((SUBLANES - n_batch - 1, d), c.dtype)], axis=0)
    mod_all = _ada(c8, ada_w, ada_b).reshape(depth, SUBLANES, 1, 6 * d)

    lat = _Rows(n_batch * seq, seq, None)
    cxt = _Rows(n_batch * n_ctx, None, n_batch)
    xs = x.reshape(n_batch * seq, d)
    cs_rows = ctx.reshape(n_batch * n_ctx, d)

    def mix_and_ffn(i, rows, h_in, p, o_attn, nb):
        mod3 = mod_all[i]
        yf = _fourier(p, cs, dft_l if rows is lat else dft_c, n_batch=nb, f_off=PX_F)
        cv = _conv(p, conv_wb[i], conv_b[i], conv_ln_g[i], conv_ln_b[i],
                   n_batch=nb, a_off=PX_A, g_off=PX_G)
        merged = _merge(yf, cv, o_attn, p, b_gate3, w_fourier_b, w_conv_out_b, w_mla_o_b, i,
                        gate_off=PX_GATE)
        mixed = _resmm(merged, w_out_b, i, h_in, mod3, rows, gate_chunk=2, tm=1024, tn=512)
        hid = _ffn_up(mixed, norm_ffn_g[i], mod3, w_gate_b, w_up_b, i, rows)
        return _resmm(hid, w_down_b, i, mixed, mod3, rows, gate_chunk=5, tm=512, tn=512)

    for i in range(depth):
        last = i == depth - 1
        mod3 = mod_all[i]
        px = _inproj(xs, norm_mix_g[i], mod3, w_in_r, i, lat, shift_chunk=0, tn=768)
        if last:
            pc = _inproj(cs_rows, norm_mix_g[i], mod3, w_in_r, i, cxt, shift_chunk=0,
                         tn=PKV_COLS, col_off=PX_KVD, n=PKV_COLS)
            kvd_off, kpe_off = PKV_KVD, PKV_KPE
        else:
            pc = _inproj(cs_rows, norm_mix_g[i], mod3, w_in_r, i, cxt, shift_chunk=0, tn=768)
            kvd_off, kpe_off = PX_KVD, PX_KPE

        q_x = _qup(px, q_norm_g[i], cos_l, sin_l, w_uq_r, i, qd_off=PX_QD)
        kv_x, kpe_x = _kvup(px, kv_norm_g[i], cos_l, sin_l, w_ukv_r, i,
                            kvd_off=PX_KVD, kpe_off=PX_KPE)
        kv_c, kpe_c = _kvup(pc, kv_norm_g[i], cos_c, sin_c, w_ukv_r, i,
                            kvd_off=kvd_off, kpe_off=kpe_off)
        o_x = _attention(q_x, kv_c, kpe_c, n_batch=n_batch, n_ctx=n_ctx,
                         q_rows_per_batch=seq, lat=(kv_x, kpe_x))
        if not last:
            q_c = _qup(pc, q_norm_g[i], cos_c, sin_c, w_uq_r, i, qd_off=PX_QD)
            o_c = _attention(q_c, kv_c, kpe_c, n_batch=n_batch, n_ctx=n_ctx,
                             q_rows_per_batch=n_ctx)
            cs_rows = mix_and_ffn(i, cxt, cs_rows, pc, o_c, n_batch)
        xs = mix_and_ffn(i, lat, xs, px, o_x, n_batch)

    return _final_norm(xs, final_norm_g).reshape(n_batch, seq, d)
```

```python
import functools

import jax
import jax.numpy as jnp
from jax import lax
from jax.experimental import pallas as pl
from jax.experimental.pallas import tpu as pltpu

F32 = jnp.float32
BF16 = jnp.bfloat16

GRID_W = 64
N_HEADS = 16
QK_NOPE = 128
QK_ROPE = 64
V_DIM = 128
F_GROUPS = 8
F_GROUP_DIM = 128
CONV_K = 31
ROPE_BASE = 10000.0
ATTN_SCALE = (QK_NOPE + QK_ROPE) ** -0.5
Q_SCALE = ATTN_SCALE * 1.4426950408889634
EPS = 1e-6

LANES = 128
SUBLANES = 8
BF16_ROWS = 16
VMEM_LIMIT = 56 * 1024 * 1024

HEAD_W = 2 * LANES
HALO = BF16_ROWS

PX_F = 0
PX_A = 1024
PX_G = 2048
PX_KVD = 3072
PX_KPE = 3584
PX_QD = 3840
PX_GATE = 4608
PX_COLS = 10752
PKV_KVD = 0
PKV_KPE = 512
PKV_COLS = 768


def _cparams(n_axes):
    return pltpu.CompilerParams(
        dimension_semantics=("arbitrary",) * n_axes, vmem_limit_bytes=VMEM_LIMIT)


def _blk(off, width):
    assert off % width == 0, (off, width)
    return off // width


def _dot(a, b):
    return jnp.dot(a, b, preferred_element_type=F32)


def _silu(v):
    return v * jax.nn.sigmoid(v)


def _rms(x, g):
    r = lax.rsqrt(jnp.mean(x * x, axis=-1, keepdims=True) + EPS)
    return (x * r) * g


def _rope128(pe, cos, sin):
    lane = lax.broadcasted_iota(jnp.int32, pe.shape, 1)
    half = QK_ROPE // 2
    rot = jnp.where(lane < half, -pltpu.roll(pe, LANES - half, 1), pltpu.roll(pe, half, 1))
    return pe * cos + rot * sin


def _ada_kernel(c_ref, w_ref, b_ref, o_ref):
    a = _silu(c_ref[...]).astype(BF16)
    o_ref[...] = _dot(a, w_ref[...].astype(BF16)) + b_ref[...]


def _ada(c8, ada_w, ada_b):
    depth, d, n6 = ada_w.shape
    tn = 1024
    return pl.pallas_call(
        _ada_kernel,
        grid=(depth, n6 // tn),
        in_specs=[
            pl.BlockSpec((SUBLANES, d), lambda l, j: (0, 0)),
            pl.BlockSpec((None, d, tn), lambda l, j: (l, 0, j)),
            pl.BlockSpec((None, 1, tn), lambda l, j: (l, 0, j)),
        ],
        out_specs=pl.BlockSpec((None, SUBLANES, tn), lambda l, j: (l, 0, j)),
        out_shape=jax.ShapeDtypeStruct((depth, SUBLANES, n6), F32),
        compiler_params=_cparams(2),
        name="ada_mod",
    )(c8, ada_w, ada_b.reshape(depth, 1, n6))


class _Rows:
    def __init__(self, m, rows_per_batch, fixed_mod_row):
        self.m = m
        self.rows_per_batch = rows_per_batch
        self.fixed_mod_row = fixed_mod_row

    def tile(self, preferred):
        return min(preferred, self.rows_per_batch or self.m)

    def mod_row(self, tm):
        if self.fixed_mod_row is not None:
            row = self.fixed_mod_row
            return lambda i: row
        rpb = self.rows_per_batch
        assert rpb % tm == 0
        return lambda i: (i * tm) // rpb


def _mod_spec(rows, tm, d, chunk):
    row = rows.mod_row(tm)
    return pl.BlockSpec((None, 1, d), lambda i, j: (row(i), 0, chunk))


def _inproj_kernel(x_ref, g_ref, sc_ref, sh_ref, w_ref, o_ref, h_ref):
    @pl.when(pl.program_id(1) == 0)
    def _():
        h = _rms(x_ref[...], g_ref[...]) * (1.0 + sc_ref[...]) + sh_ref[...]
        h_ref[...] = h.astype(BF16)

    o_ref[...] = _dot(h_ref[...], w_ref[...]).astype(o_ref.dtype)


def _wspec(w, layer, tn, col0=0):
    return pl.BlockSpec((None, w.shape[1], tn), lambda i, j: (layer, 0, col0 + j))


def _inproj(x, g, mod3, w, layer, rows, *, shift_chunk, tn, col_off=0, n=None):
    m, d = x.shape
    n = n or w.shape[2]
    tm = rows.tile(1024)
    return pl.pallas_call(
        _inproj_kernel,
        grid=(m // tm, n // tn),
        in_specs=[
            pl.BlockSpec((tm, d), lambda i, j: (i, 0)),
            pl.BlockSpec((1, d), lambda i, j: (0, 0)),
            _mod_spec(rows, tm, d, shift_chunk + 1),
            _mod_spec(rows, tm, d, shift_chunk),
            _wspec(w, layer, tn, _blk(col_off, tn)),
        ],
        out_specs=pl.BlockSpec((tm, tn), lambda i, j: (i, j)),
        out_shape=jax.ShapeDtypeStruct((m, n), BF16),
        scratch_shapes=[pltpu.VMEM((tm, d), BF16)],
        compiler_params=_cparams(2),
        name="norm_mod_proj",
    )(x, g.reshape(1, d), mod3, mod3, w)


def _ffn_up_kernel(x_ref, g_ref, sc_ref, sh_ref, wg_ref, wu_ref, o_ref, h_ref):
    @pl.when(pl.program_id(1) == 0)
    def _():
        h = _rms(x_ref[...], g_ref[...]) * (1.0 + sc_ref[...]) + sh_ref[...]
        h_ref[...] = h.astype(BF16)

    hb = h_ref[...]
    o_ref[...] = (_silu(_dot(hb, wg_ref[...])) * _dot(hb, wu_ref[...])).astype(o_ref.dtype)


def _ffn_up(x, g, mod3, wg, wu, layer, rows):
    m, d = x.shape
    n = wg.shape[2]
    tm = rows.tile(1024)
    tn = 512
    return pl.pallas_call(
        _ffn_up_kernel,
        grid=(m // tm, n // tn),
        in_specs=[
            pl.BlockSpec((tm, d), lambda i, j: (i, 0)),
            pl.BlockSpec((1, d), lambda i, j: (0, 0)),
            _mod_spec(rows, tm, d, 4),
            _mod_spec(rows, tm, d, 3),
            _wspec(wg, layer, tn),
            _wspec(wu, layer, tn),
        ],
        out_specs=pl.BlockSpec((tm, tn), lambda i, j: (i, j)),
        out_shape=jax.ShapeDtypeStruct((m, n), BF16),
        scratch_shapes=[pltpu.VMEM((tm, d), BF16)],
        compiler_params=_cparams(2),
        name="ffn_gate_up",
    )(x, g.reshape(1, d), mod3, mod3, wg, wu)


def _resmm_kernel(a_ref, w_ref, res_ref, gate_ref, o_ref):
    o_ref[...] = res_ref[...] + gate_ref[...] * _dot(a_ref[...], w_ref[...])


def _resmm(a, w, layer, res, mod3, rows, *, gate_chunk, tm, tn):
    m, k = a.shape
    d = w.shape[2]
    tm = rows.tile(tm)
    row = rows.mod_row(tm)
    per_chunk = d // tn
    return pl.pallas_call(
        _resmm_kernel,
        grid=(m // tm, d // tn),
        in_specs=[
            pl.BlockSpec((tm, k), lambda i, j: (i, 0)),
            _wspec(w, layer, tn),
            pl.BlockSpec((tm, tn), lambda i, j: (i, j)),
            pl.BlockSpec((None, 1, tn), lambda i, j: (row(i), 0, gate_chunk * per_chunk + j)),
        ],
        out_specs=pl.BlockSpec((tm, tn), lambda i, j: (i, j)),
        out_shape=jax.ShapeDtypeStruct((m, d), F32),
        compiler_params=_cparams(2),
        name="proj_gated_residual",
    )(a, w, res, mod3)


def _qup_kernel(qd_ref, g_ref, cos_ref, sin_ref, w_ref, o_ref, h_ref, *, heads_per_tile):
    @pl.when(pl.program_id(1) == 0)
    def _():
        h_ref[...] = _rms(qd_ref[...].astype(F32), g_ref[...]).astype(BF16)

    acc = _dot(h_ref[...], w_ref[...]) * Q_SCALE
    cos = cos_ref[...]
    sin = sin_ref[...]
    for hh in range(heads_per_tile):
        c0 = hh * HEAD_W
        o_ref[:, c0:c0 + LANES] = acc[:, c0:c0 + LANES].astype(o_ref.dtype)
        pe = acc[:, c0 + LANES:c0 + HEAD_W]
        o_ref[:, c0 + LANES:c0 + HEAD_W] = _rope128(pe, cos, sin).astype(o_ref.dtype)


def _qup(p, g, cos, sin, w, layer, *, qd_off):
    m = p.shape[0]
    _, kq, n = w.shape
    tm = min(1024, m)
    tn = 2048
    return pl.pallas_call(
        functools.partial(_qup_kernel, heads_per_tile=tn // HEAD_W),
        grid=(m // tm, n // tn),
        in_specs=[
            pl.BlockSpec((tm, kq), lambda i, j: (i, _blk(qd_off, kq))),
            pl.BlockSpec((1, kq), lambda i, j: (0, 0)),
            pl.BlockSpec((tm, LANES), lambda i, j: (i, 0)),
            pl.BlockSpec((tm, LANES), lambda i, j: (i, 0)),
            _wspec(w, layer, tn),
        ],
        out_specs=pl.BlockSpec((tm, tn), lambda i, j: (i, j)),
        out_shape=jax.ShapeDtypeStruct((m, n), BF16),
        scratch_shapes=[pltpu.VMEM((tm, kq), BF16)],
        compiler_params=_cparams(2),
        name="mla_q_up",
    )(p, g.reshape(1, kq), cos, sin, w)


def _kvup_kernel(kvd_ref, g_ref, kpe_ref, cos_ref, sin_ref, w_ref, o_ref, okpe_ref, h_ref):
    @pl.when(pl.program_id(1) == 0)
    def _():
        h_ref[...] = _rms(kvd_ref[...].astype(F32), g_ref[...]).astype(BF16)
        pe = kpe_ref[...].astype(F32)
        okpe_ref[...] = _rope128(pe, cos_ref[...], sin_ref[...]).astype(okpe_ref.dtype)

    o_ref[...] = _dot(h_ref[...], w_ref[...]).astype(o_ref.dtype)


def _kvup(p, g, cos, sin, w, layer, *, kvd_off, kpe_off):
    m = p.shape[0]
    _, kk, n = w.shape
    tm = min(1024, m)
    tn = 2048
    return pl.pallas_call(
        _kvup_kernel,
        grid=(m // tm, n // tn),
        in_specs=[
            pl.BlockSpec((tm, kk), lambda i, j: (i, _blk(kvd_off, kk))),
            pl.BlockSpec((1, kk), lambda i, j: (0, 0)),
            pl.BlockSpec((tm, LANES), lambda i, j: (i, _blk(kpe_off, LANES))),
            pl.BlockSpec((tm, LANES), lambda i, j: (i, 0)),
            pl.BlockSpec((tm, LANES), lambda i, j: (i, 0)),
            _wspec(w, layer, tn),
        ],
        out_specs=[
            pl.BlockSpec((tm, tn), lambda i, j: (i, j)),
            pl.BlockSpec((tm, LANES), lambda i, j: (i, 0)),
        ],
        out_shape=[
            jax.ShapeDtypeStruct((m, n), BF16),
            jax.ShapeDtypeStruct((m, LANES), BF16),
        ],
        scratch_shapes=[pltpu.VMEM((tm, kk), BF16)],
        compiler_params=_cparams(2),
        name="mla_kv_up",
    )(p, g.reshape(1, kk), p, cos, sin, w)


ATTN_TK = HEAD_W
ATTN_SUB = 256
ATTN_TQ = 1024


def _attn_kernel(*refs, n_lat, n_ctx):
    if n_lat:
        q_ref, kl_ref, kpl_ref, vl_ref, kc_ref, kpc_ref, vc_ref, o_ref, kt_scr, s_scr = refs
        segments = ((kl_ref, kpl_ref, vl_ref, n_lat), (kc_ref, kpc_ref, vc_ref, n_ctx))
    else:
        q_ref, kc_ref, kpc_ref, vc_ref, o_ref, kt_scr, s_scr = refs
        segments = ((kc_ref, kpc_ref, vc_ref, n_ctx),)
    tk = ATTN_TK
    chunks = []
    base = 0
    for k_ref, kp_ref, v_ref, n in segments:
        assert n % tk == 0
        chunks += [(base + r, k_ref, kp_ref, v_ref, r) for r in range(0, n, tk)]
        base += n

    @pl.when(pl.program_id(2) == 0)
    def _():
        for off, k_ref, kp_ref, _, r in chunks:
            kt_scr[0:LANES, off:off + tk] = k_ref[r:r + tk, :].T
            kt_scr[LANES:HEAD_W, off:off + tk] = kp_ref[r:r + tk, :].T

    tq = q_ref.shape[0]
    sub = min(ATTN_SUB, tq)
    for r0 in range(0, tq, sub):
        rows = slice(r0, r0 + sub)
        q = q_ref[rows, :]
        m128 = None
        for off, *_ in chunks:
            s = _dot(q, kt_scr[:, off:off + tk])
            s_scr[rows, off:off + tk] = s
            mx = jnp.maximum(s[:, :LANES], s[:, LANES:])
            m128 = mx if m128 is None else jnp.maximum(m128, mx)
        mb = jnp.broadcast_to(jnp.max(m128, axis=-1, keepdims=True), (sub, LANES))

        l128 = jnp.zeros((sub, LANES), F32)
        acc = jnp.zeros((sub, V_DIM), F32)
        for off, _, _, v_ref, r in chunks:
            p0 = jnp.exp2(s_scr[rows, off:off + LANES] - mb)
            p1 = jnp.exp2(s_scr[rows, off + LANES:off + tk] - mb)
            l128 = l128 + (p0 + p1)
            p = jnp.concatenate([p0, p1], axis=1).astype(BF16)
            acc = acc + _dot(p, v_ref[r:r + tk, :])
        l = jnp.sum(l128, axis=-1, keepdims=True)
        o_ref[rows, :] = (acc / l).astype(o_ref.dtype)


def _attention(q, kv_c, kpe_c, *, n_batch, n_ctx, q_rows_per_batch, lat=None):
    m = q.shape[0]
    nq_rows = q_rows_per_batch
    tq = min(ATTN_TQ, nq_rows)
    nq = nq_rows // tq

    def q_map(b, h, i):
        return b * nq + i, h

    def k_map(b, h, i):
        return b, h

    def kpe_map(b, h, i):
        return b, 0

    def v_map(b, h, i):
        return b, N_HEADS + h

    n_lat = 0
    in_specs = [pl.BlockSpec((tq, HEAD_W), q_map)]
    args = [q]
    if lat is not None:
        kv_l, kpe_l = lat
        n_lat = kv_l.shape[0] // n_batch
        in_specs += [
            pl.BlockSpec((n_lat, LANES), k_map),
            pl.BlockSpec((n_lat, LANES), kpe_map),
            pl.BlockSpec((n_lat, LANES), v_map),
        ]
        args += [kv_l, kpe_l, kv_l]
    in_specs += [
        pl.BlockSpec((n_ctx, LANES), k_map),
        pl.BlockSpec((n_ctx, LANES), kpe_map),
        pl.BlockSpec((n_ctx, LANES), v_map),
    ]
    args += [kv_c, kpe_c, kv_c]
    n_keys = n_lat + n_ctx
    return pl.pallas_call(
        functools.partial(_attn_kernel, n_lat=n_lat, n_ctx=n_ctx),
        grid=(n_batch, N_HEADS, nq),
        in_specs=in_specs,
        out_specs=pl.BlockSpec((tq, V_DIM), q_map),
        out_shape=jax.ShapeDtypeStruct((m, N_HEADS * V_DIM), BF16),
        scratch_shapes=[
            pltpu.VMEM((HEAD_W, n_keys), BF16),
            pltpu.VMEM((tq, n_keys), F32),
        ],
        compiler_params=_cparams(3),
        name="mla_attention",
    )(*args)


def _chan_dft_kernel(u_ref, cs_ref, o_ref):
    cs = cs_ref[...]
    for grp in range(F_GROUPS):
        c0 = grp * F_GROUP_DIM
        r = _dot(u_ref[:, c0:c0 + F_GROUP_DIM], cs)
        o_ref[0, :, c0:c0 + F_GROUP_DIM] = r[:, :F_GROUP_DIM].astype(o_ref.dtype)
        o_ref[1, :, c0:c0 + F_GROUP_DIM] = r[:, F_GROUP_DIM:].astype(o_ref.dtype)


def _mm_kernel(a_ref, b_ref, o_ref):
    o_ref[...] = _dot(a_ref[...], b_ref[...]).astype(o_ref.dtype)


def _fourier(p, cs, dft, *, n_batch, f_off):
    m = p.shape[0]
    n = m // n_batch
    fw = F_GROUPS * F_GROUP_DIM
    tm = min(512, n)
    nt = n // tm
    z = pl.pallas_call(
        _chan_dft_kernel,
        grid=(n_batch, nt),
        in_specs=[
            pl.BlockSpec((tm, fw), lambda b, t: (b * nt + t, _blk(f_off, fw))),
            pl.BlockSpec((F_GROUP_DIM, 2 * F_GROUP_DIM), lambda b, t: (0, 0)),
        ],
        out_specs=pl.BlockSpec((None, 2, tm, fw), lambda b, t: (b, 0, t, 0)),
        out_shape=jax.ShapeDtypeStruct((n_batch, 2, n, fw), BF16),
        compiler_params=_cparams(2),
        name="fourier_channel_dft",
    )(p, cs)
    z = z.reshape(n_batch, 2 * n, fw)
    tn = 512
    return pl.pallas_call(
        _mm_kernel,
        grid=(nt, n_batch, fw // tn),
        in_specs=[
            pl.BlockSpec((tm, 2 * n), lambda i, b, j: (i, 0)),
            pl.BlockSpec((None, 2 * n, tn), lambda i, b, j: (b, 0, j)),
        ],
        out_specs=pl.BlockSpec((tm, tn), lambda i, b, j: (b * nt + i, j)),
        out_shape=jax.ShapeDtypeStruct((m, fw), BF16),
        compiler_params=_cparams(3),
        name="fourier_position_dft",
    )(dft, z)


CONV_ROWS = 64


def _conv_kernel(a_ref, g_ref, ap_ref, gp_ref, an_ref, gn_ref, wb_ref, cb_ref, lg_ref, lb_ref,
                 o_ref, buf, shifted, conv_scr, *, ts):
    t = pl.program_id(1)
    last = pl.num_programs(1) - 1

    def glu(a, g):
        return a.astype(F32) * jax.nn.sigmoid(g.astype(F32))

    zeros = jnp.zeros((HALO, buf.shape[1]), F32)
    buf[0:HALO, :] = jnp.where(t > 0, glu(ap_ref[...], gp_ref[...]), zeros)
    buf[HALO:HALO + ts, :] = glu(a_ref[...], g_ref[...])
    buf[HALO + ts:2 * HALO + ts, :] = jnp.where(t < last, glu(an_ref[...], gn_ref[...]), zeros)

    span = ts + 3 * SUBLANES
    for s in range(1, SUBLANES):
        shifted[s - 1, 0:span, :] = buf[s:s + span, :]

    n_lane_chunks = buf.shape[1] // LANES
    groups = CONV_ROWS // SUBLANES

    def body(rc, carry):
        r0 = pl.multiple_of(rc * CONV_ROWS, CONV_ROWS)
        for lc in range(n_lane_chunks):
            ls = slice(lc * LANES, (lc + 1) * LANES)
            accs = [jnp.zeros((SUBLANES, LANES), F32) for _ in range(groups)]
            for k in range(CONV_K):
                whole, part = divmod(k + HALO - CONV_K // 2, SUBLANES)
                w = wb_ref[k * SUBLANES:(k + 1) * SUBLANES, ls]
                for j in range(groups):
                    row = r0 + (whole + j) * SUBLANES
                    if part == 0:
                        x = buf[pl.ds(row, SUBLANES), ls]
                    else:
                        x = shifted[part - 1, pl.ds(row, SUBLANES), ls]
                    accs[j] = accs[j] + w * x
            bias = cb_ref[:, ls]
            for j in range(groups):
                conv_scr[pl.ds(r0 + j * SUBLANES, SUBLANES), ls] = accs[j] + bias
        return carry

    lax.fori_loop(0, ts // CONV_ROWS, body, 0)

    v = conv_scr[...]
    mu = jnp.mean(v, axis=-1, keepdims=True)
    dv = v - mu
    var = jnp.mean(dv * dv, axis=-1, keepdims=True)
    y = dv * lax.rsqrt(var + EPS) * lg_ref[...] + lb_ref[...]
    o_ref[...] = _silu(y).astype(o_ref.dtype)


def _conv(p, wb, cb, lg, lb, *, n_batch, a_off, g_off):
    m = p.shape[0]
    cw = cb.shape[0]
    n = m // n_batch
    ts = min(256, n)
    nt = n // ts
    halo_per_tile = ts // HALO
    last_halo = m // HALO - 1

    def main(off):
        return pl.BlockSpec((ts, cw), lambda b, t: (b * nt + t, _blk(off, cw)))

    def prev(off):
        return pl.BlockSpec(
            (HALO, cw), lambda b, t: (jnp.maximum((b * nt + t) * halo_per_tile - 1, 0), _blk(off, cw)))

    def nxt(off):
        return pl.BlockSpec(
            (HALO, cw),
            lambda b, t: (jnp.minimum((b * nt + t + 1) * halo_per_tile, last_halo), _blk(off, cw)))

    vec = pl.BlockSpec((1, cw), lambda b, t: (0, 0))
    return pl.pallas_call(
        functools.partial(_conv_kernel, ts=ts),
        grid=(n_batch, nt),
        in_specs=[
            main(a_off), main(g_off), prev(a_off), prev(g_off), nxt(a_off), nxt(g_off),
            pl.BlockSpec((CONV_K * SUBLANES, cw), lambda b, t: (0, 0)),
            vec, vec, vec,
        ],
        out_specs=pl.BlockSpec((ts, cw), lambda b, t: (b * nt + t, 0)),
        out_shape=jax.ShapeDtypeStruct((m, cw), BF16),
        scratch_shapes=[
            pltpu.VMEM((ts + 2 * HALO, cw), F32),
            pltpu.VMEM((SUBLANES - 1, ts + 3 * SUBLANES, cw), F32),
            pltpu.VMEM((ts, cw), F32),
        ],
        compiler_params=_cparams(2),
        name="conformer_conv",
    )(p, p, p, p, p, p, wb, cb.reshape(1, cw), lg.reshape(1, cw), lb.reshape(1, cw))


def _merge_kernel(yf_ref, cv_ref, oa_ref, gf_ref, gc_ref, ga_ref, bf_ref, bc_ref, ba_ref,
                  wf_ref, wc_ref, wa_ref, o_ref):
    def gate(g_ref, b_ref):
        return jax.nn.sigmoid(g_ref[...].astype(F32) + b_ref[...])

    acc = gate(gf_ref, bf_ref) * _dot(yf_ref[...], wf_ref[...])
    acc = acc + gate(gc_ref, bc_ref) * _dot(cv_ref[...], wc_ref[...])
    acc = acc + gate(ga_ref, ba_ref) * _dot(oa_ref[...], wa_ref[...])
    o_ref[...] = acc.astype(o_ref.dtype)


def _merge(yf, cv, oa, p, b_gate, wf, wc, wa, layer, *, gate_off):
    m = yf.shape[0]
    d = wf.shape[2]
    tm = min(1024, m)
    tn = 512
    nj = d // tn
    g0 = _blk(gate_off, tn)

    def act(a):
        return pl.BlockSpec((tm, a.shape[1]), lambda i, j: (i, 0))

    def gate(br):
        return pl.BlockSpec((tm, tn), lambda i, j: (i, g0 + br * nj + j))

    def bias(br):
        return pl.BlockSpec((None, 1, tn), lambda i, j: (layer, 0, br * nj + j))

    def weight(w):
        return _wspec(w, layer, tn)

    return pl.pallas_call(
        _merge_kernel,
        grid=(m // tm, nj),
        in_specs=[act(yf), act(cv), act(oa), gate(0), gate(1), gate(2),
                  bias(0), bias(1), bias(2), weight(wf), weight(wc), weight(wa)],
        out_specs=pl.BlockSpec((tm, tn), lambda i, j: (i, j)),
        out_shape=jax.ShapeDtypeStruct((m, d), BF16),
        compiler_params=_cparams(2),
        name="branch_merge",
    )(yf, cv, oa, p, p, p, b_gate, b_gate, b_gate, wf, wc, wa)


def _final_norm_kernel(x_ref, g_ref, o_ref):
    o_ref[...] = _rms(x_ref[...], g_ref[...])


def _final_norm(x, g):
    m, d = x.shape
    tm = min(512, m)
    return pl.pallas_call(
        _final_norm_kernel,
        grid=(m // tm,),
        in_specs=[pl.BlockSpec((tm, d), lambda i: (i, 0)), pl.BlockSpec((1, d), lambda i: (0, 0))],
        out_specs=pl.BlockSpec((tm, d), lambda i: (i, 0)),
        out_shape=jax.ShapeDtypeStruct((m, d), F32),
        compiler_params=_cparams(1),
        name="final_rmsnorm",
    )(x, g.reshape(1, d))


def _rope_tables(n_tok):
    rows = n_tok // GRID_W
    row = jnp.broadcast_to(jnp.arange(rows, dtype=F32)[:, None], (rows, GRID_W)).reshape(-1)
    col = jnp.broadcast_to(jnp.arange(GRID_W, dtype=F32)[None, :], (rows, GRID_W)).reshape(-1)
    n_freq = QK_ROPE // 4
    inv_freq = ROPE_BASE ** (-jnp.arange(n_freq, dtype=F32) / n_freq)
    ang = jnp.concatenate([row[:, None] * inv_freq, col[:, None] * inv_freq], axis=-1)
    pad = jnp.zeros((n_tok, LANES - QK_ROPE), F32)
    cos = jnp.concatenate([jnp.cos(ang), jnp.cos(ang), pad], axis=-1)
    sin = jnp.concatenate([jnp.sin(ang), jnp.sin(ang), pad], axis=-1)
    return cos, sin


def _dft_cos_sin(n):
    n2 = min(n, GRID_W)
    n1 = n // n2
    k = jnp.arange(n, dtype=jnp.int32)[:, None]
    ang1 = (((k % n1) * jnp.arange(n1, dtype=jnp.int32)[None, :]) % n1).astype(F32) * (2.0 * jnp.pi / n1)
    ang2 = ((k * jnp.arange(n2, dtype=jnp.int32)[None, :]) % n).astype(F32) * (2.0 * jnp.pi / n)
    scale = n ** -0.5
    c1, s1 = jnp.cos(ang1)[:, :, None] * scale, jnp.sin(ang1)[:, :, None] * scale
    c2, s2 = jnp.cos(ang2)[:, None, :], jnp.sin(ang2)[:, None, :]
    return (c1 * c2 - s1 * s2).reshape(n, n), (s1 * c2 + c1 * s2).reshape(n, n)


def kernel(x, c, ctx, c_ctx, ada_w, ada_b, norm_mix_g, w_in, b_gate, w_fourier, conv_w, conv_b,
           conv_ln_g, conv_ln_b, w_conv_out, q_norm_g, w_uq, kv_norm_g, w_ukv, w_mla_o, w_out,
           norm_ffn_g, w_ffn_gate, w_ffn_up, w_ffn_down, final_norm_g):
    n_batch, seq, d = x.shape
    n_ctx = ctx.shape[1]
    depth = ada_w.shape[0]
    assert n_batch < SUBLANES
    q_lora = q_norm_g.shape[1]
    kv_lora = kv_norm_g.shape[1]
    f_width = w_fourier.shape[1]
    c_width = conv_w.shape[2]
    assert (f_width, c_width, q_lora, kv_lora) == (PX_A - PX_F, PX_G - PX_A, PX_GATE - PX_QD,
                                                    PX_KPE - PX_KVD)
    f_end = f_width
    c_end = f_end + 2 * c_width
    q_end = c_end + q_lora
    kv_end = q_end + kv_lora
    kpe_end = kv_end + QK_ROPE

    zeros = lambda k: jnp.zeros((depth, d, k), w_in.dtype)
    w_in_r = jnp.concatenate([
        w_in[:, :, :c_end],
        w_in[:, :, q_end:kv_end],
        w_in[:, :, kv_end:kpe_end], zeros(LANES - QK_ROPE),
        zeros(PX_QD - PX_KPE - LANES),
        w_in[:, :, c_end:q_end],
        w_in[:, :, kpe_end:],
    ], axis=-1).astype(BF16)
    assert w_in_r.shape[-1] == PX_COLS
    w_uq_h = w_uq.reshape(depth, q_lora, N_HEADS, QK_NOPE + QK_ROPE)
    w_uq_r = jnp.concatenate(
        [w_uq_h, jnp.zeros((depth, q_lora, N_HEADS, HEAD_W - QK_NOPE - QK_ROPE), w_uq.dtype)],
        axis=-1).reshape(depth, q_lora, N_HEADS * HEAD_W).astype(BF16)
    w_ukv_r = w_ukv.reshape(depth, kv_lora, N_HEADS, 2, QK_NOPE).transpose(0, 1, 3, 2, 4).reshape(
        depth, kv_lora, 2 * N_HEADS * QK_NOPE).astype(BF16)
    w_fourier_b = w_fourier.astype(BF16)
    w_conv_out_b = w_conv_out.astype(BF16)
    w_mla_o_b = w_mla_o.astype(BF16)
    w_out_b = w_out.astype(BF16)
    w_gate_b = w_ffn_gate.astype(BF16)
    w_up_b = w_ffn_up.astype(BF16)
    w_down_b = w_ffn_down.astype(BF16)
    conv_wb = jnp.repeat(conv_w, SUBLANES, axis=1)
    b_gate3 = b_gate.reshape(depth, 1, -1)

    cos_l, sin_l = _rope_tables(seq)
    cos_l = jnp.tile(cos_l, (n_batch, 1))
    sin_l = jnp.tile(sin_l, (n_batch, 1))
    cos_c = jnp.ones((n_batch * n_ctx, LANES), F32)
    sin_c = jnp.zeros((n_batch * n_ctx, LANES), F32)
    c128, s128 = _dft_cos_sin(F_GROUP_DIM)
    cs = jnp.concatenate([c128, s128], axis=1).astype(BF16)
    cn, sn = _dft_cos_sin(seq)
    dft_l = jnp.concatenate([cn, -sn], axis=1).astype(BF16)
    cc, sc = _dft_cos_sin(n_ctx)
    dft_c = jnp.concatenate([cc, -sc], axis=1).astype(BF16)

    c8 = jnp.concatenate(
        [c, c_ctx[None, :], jnp.zeros((SUBLANES - n_batch - 1, d), c.dtype)], axis=0)
    mod_all = _ada(c8, ada_w, ada_b).reshape(depth, SUBLANES, 1, 6 * d)

    lat = _Rows(n_batch * seq, seq, None)
    cxt = _Rows(n_batch * n_ctx, None, n_batch)
    xs = x.reshape(n_batch * seq, d)
    cs_rows = ctx.reshape(n_batch * n_ctx, d)

    def mix_and_ffn(i, rows, h_in, p, o_attn, nb):
        mod3 = mod_all[i]
        yf = _fourier(p, cs, dft_l if rows is lat else dft_c, n_batch=nb, f_off=PX_F)
        cv = _conv(p, conv_wb[i], conv_b[i], conv_ln_g[i], conv_ln_b[i],
                   n_batch=nb, a_off=PX_A, g_off=PX_G)
        merged = _merge(yf, cv, o_attn, p, b_gate3, w_fourier_b, w_conv_out_b, w_mla_o_b, i,
                        gate_off=PX_GATE)
        mixed = _resmm(merged, w_out_b, i, h_in, mod3, rows, gate_chunk=2, tm=2048, tn=512)
        hid = _ffn_up(mixed, norm_ffn_g[i], mod3, w_gate_b, w_up_b, i, rows)
        return _resmm(hid, w_down_b, i, mixed, mod3, rows, gate_chunk=5, tm=1024, tn=512)

    for i in range(depth):
        last = i == depth - 1
        mod3 = mod_all[i]
        px = _inproj(xs, norm_mix_g[i], mod3, w_in_r, i, lat, shift_chunk=0, tn=1536)
        if last:
            pc = _inproj(cs_rows, norm_mix_g[i], mod3, w_in_r, i, cxt, shift_chunk=0,
                         tn=PKV_COLS, col_off=PX_KVD, n=PKV_COLS)
            kvd_off, kpe_off = PKV_KVD, PKV_KPE
        else:
            pc = _inproj(cs_rows, norm_mix_g[i], mod3, w_in_r, i, cxt, shift_chunk=0, tn=1536)
            kvd_off, kpe_off = PX_KVD, PX_KPE

        q_x = _qup(px, q_norm_g[i], cos_l, sin_l, w_uq_r, i, qd_off=PX_QD)
        kv_x, kpe_x = _kvup(px, kv_norm_g[i], cos_l, sin_l, w_ukv_r, i,
                            kvd_off=PX_KVD, kpe_off=PX_KPE)
        kv_c, kpe_c = _kvup(pc, kv_norm_g[i], cos_c, sin_c, w_ukv_r, i,
                            kvd_off=kvd_off, kpe_off=kpe_off)
        o_x = _attention(q_x, kv_c, kpe_c, n_batch=n_batch, n_ctx=n_ctx,
                         q_rows_per_batch=seq, lat=(kv_x, kpe_x))
        if not last:
            q_c = _qup(pc, q_norm_g[i], cos_c, sin_c, w_uq_r, i, qd_off=PX_QD)
            o_c = _attention(q_c, kv_c, kpe_c, n_batch=n_batch, n_ctx=n_ctx,
                             q_rows_per_batch=n_ctx)
            cs_rows = mix_and_ffn(i, cxt, cs_rows, pc, o_c, n_batch)
        xs = mix_and_ffn(i, lat, xs, px, o_x, n_batch)

    return _final_norm(xs, final_norm_g).reshape(n_batch, seq, d)
```

```python
import functools

import jax
import jax.numpy as jnp
from jax import lax
from jax.experimental import pallas as pl
from jax.experimental.pallas import tpu as pltpu

F32 = jnp.float32
BF16 = jnp.bfloat16

GRID_W = 64
N_HEADS = 16
QK_NOPE = 128
QK_ROPE = 64
V_DIM = 128
F_GROUPS = 8
F_GROUP_DIM = 128
CONV_K = 31
ROPE_BASE = 10000.0
ATTN_SCALE = (QK_NOPE + QK_ROPE) ** -0.5
Q_SCALE = ATTN_SCALE * 1.4426950408889634
EPS = 1e-6

LANES = 128
SUBLANES = 8
BF16_ROWS = 16
VMEM_LIMIT = 56 * 1024 * 1024

HEAD_W = 2 * LANES
HALO = BF16_ROWS

PX_F = 0
PX_A = 1024
PX_G = 2048
PX_KVD = 3072
PX_KPE = 3584
PX_QD = 3840
PX_GATE = 4608
PX_COLS = 10752
PKV_KVD = 0
PKV_KPE = 512
PKV_COLS = 768


def _cparams(n_axes):
    return pltpu.CompilerParams(
        dimension_semantics=("arbitrary",) * n_axes, vmem_limit_bytes=VMEM_LIMIT)


def _blk(off, width):
    assert off % width == 0, (off, width)
    return off // width


def _dot(a, b):
    return jnp.dot(a, b, preferred_element_type=F32)


def _silu(v):
    return v * jax.nn.sigmoid(v)


def _rms(x, g):
    r = lax.rsqrt(jnp.mean(x * x, axis=-1, keepdims=True) + EPS)
    return (x * r) * g


ROW_CHUNK_UNROLL = 8


def _by_row_chunks(n_rows, fn):
    def body(c, carry):
        fn(pl.ds(pl.multiple_of(c * BF16_ROWS, BF16_ROWS), BF16_ROWS))
        return carry

    lax.fori_loop(0, n_rows // BF16_ROWS, body, 0, unroll=ROW_CHUNK_UNROLL)


def _rope128(pe, cos, sin):
    lane = lax.broadcasted_iota(jnp.int32, pe.shape, 1)
    half = QK_ROPE // 2
    rot = jnp.where(lane < half, -pltpu.roll(pe, LANES - half, 1), pltpu.roll(pe, half, 1))
    return pe * cos + rot * sin


def _ada_kernel(c_ref, w_ref, b_ref, o_ref):
    a = _silu(c_ref[...]).astype(BF16)
    o_ref[...] = _dot(a, w_ref[...].astype(BF16)) + b_ref[...]


def _ada(c8, ada_w, ada_b):
    depth, d, n6 = ada_w.shape
    tn = 1024
    return pl.pallas_call(
        _ada_kernel,
        grid=(depth, n6 // tn),
        in_specs=[
            pl.BlockSpec((SUBLANES, d), lambda l, j: (0, 0)),
            pl.BlockSpec((None, d, tn), lambda l, j: (l, 0, j)),
            pl.BlockSpec((None, 1, tn), lambda l, j: (l, 0, j)),
        ],
        out_specs=pl.BlockSpec((None, SUBLANES, tn), lambda l, j: (l, 0, j)),
        out_shape=jax.ShapeDtypeStruct((depth, SUBLANES, n6), F32),
        compiler_params=_cparams(2),
        name="ada_mod",
    )(c8, ada_w, ada_b.reshape(depth, 1, n6))


class _Rows:
    def __init__(self, m, rows_per_batch, fixed_mod_row):
        self.m = m
        self.rows_per_batch = rows_per_batch
        self.fixed_mod_row = fixed_mod_row

    def tile(self, preferred):
        return min(preferred, self.rows_per_batch or self.m)

    def mod_row(self, tm):
        if self.fixed_mod_row is not None:
            row = self.fixed_mod_row
            return lambda i: row
        rpb = self.rows_per_batch
        assert rpb % tm == 0
        return lambda i: (i * tm) // rpb


def _mod_spec(rows, tm, d, chunk):
    row = rows.mod_row(tm)
    return pl.BlockSpec((None, 1, d), lambda i, j: (row(i), 0, chunk))


def _norm_modulate(x_ref, g_ref, sc_ref, sh_ref, h_ref):
    def chunk(rows):
        h = _rms(x_ref[rows, :], g_ref[...]) * (1.0 + sc_ref[...]) + sh_ref[...]
        h_ref[rows, :] = h.astype(h_ref.dtype)

    _by_row_chunks(x_ref.shape[0], chunk)


def _inproj_kernel(x_ref, g_ref, sc_ref, sh_ref, w_ref, o_ref, h_ref):
    @pl.when(pl.program_id(1) == 0)
    def _():
        _norm_modulate(x_ref, g_ref, sc_ref, sh_ref, h_ref)

    o_ref[...] = _dot(h_ref[...], w_ref[...]).astype(o_ref.dtype)


def _wspec(w, layer, tn, col0=0):
    return pl.BlockSpec((None, w.shape[1], tn), lambda i, j: (layer, 0, col0 + j))


def _inproj(x, g, mod3, w, layer, rows, *, shift_chunk, tn, col_off=0, n=None):
    m, d = x.shape
    n = n or w.shape[2]
    tm = rows.tile(1024)
    return pl.pallas_call(
        _inproj_kernel,
        grid=(m // tm, n // tn),
        in_specs=[
            pl.BlockSpec((tm, d), lambda i, j: (i, 0)),
            pl.BlockSpec((1, d), lambda i, j: (0, 0)),
            _mod_spec(rows, tm, d, shift_chunk + 1),
            _mod_spec(rows, tm, d, shift_chunk),
            _wspec(w, layer, tn, _blk(col_off, tn)),
        ],
        out_specs=pl.BlockSpec((tm, tn), lambda i, j: (i, j)),
        out_shape=jax.ShapeDtypeStruct((m, n), BF16),
        scratch_shapes=[pltpu.VMEM((tm, d), BF16)],
        compiler_params=_cparams(2),
        name="norm_mod_proj",
    )(x, g.reshape(1, d), mod3, mod3, w)


def _ffn_up_kernel(x_ref, g_ref, sc_ref, sh_ref, wg_ref, wu_ref, o_ref, h_ref):
    @pl.when(pl.program_id(1) == 0)
    def _():
        _norm_modulate(x_ref, g_ref, sc_ref, sh_ref, h_ref)

    hb = h_ref[...]
    o_ref[...] = (_silu(_dot(hb, wg_ref[...])) * _dot(hb, wu_ref[...])).astype(o_ref.dtype)


def _ffn_up(x, g, mod3, wg, wu, layer, rows):
    m, d = x.shape
    n = wg.shape[2]
    tm = rows.tile(1024)
    tn = 512
    return pl.pallas_call(
        _ffn_up_kernel,
        grid=(m // tm, n // tn),
        in_specs=[
            pl.BlockSpec((tm, d), lambda i, j: (i, 0)),
            pl.BlockSpec((1, d), lambda i, j: (0, 0)),
            _mod_spec(rows, tm, d, 4),
            _mod_spec(rows, tm, d, 3),
            _wspec(wg, layer, tn),
            _wspec(wu, layer, tn),
        ],
        out_specs=pl.BlockSpec((tm, tn), lambda i, j: (i, j)),
        out_shape=jax.ShapeDtypeStruct((m, n), BF16),
        scratch_shapes=[pltpu.VMEM((tm, d), BF16)],
        compiler_params=_cparams(2),
        name="ffn_gate_up",
    )(x, g.reshape(1, d), mod3, mod3, wg, wu)


def _resmm_kernel(a_ref, w_ref, res_ref, gate_ref, o_ref):
    o_ref[...] = res_ref[...] + gate_ref[...] * _dot(a_ref[...], w_ref[...])


def _resmm(a, w, layer, res, mod3, rows, *, gate_chunk, tm, tn):
    m, k = a.shape
    d = w.shape[2]
    tm = rows.tile(tm)
    row = rows.mod_row(tm)
    per_chunk = d // tn
    return pl.pallas_call(
        _resmm_kernel,
        grid=(m // tm, d // tn),
        in_specs=[
            pl.BlockSpec((tm, k), lambda i, j: (i, 0)),
            _wspec(w, layer, tn),
            pl.BlockSpec((tm, tn), lambda i, j: (i, j)),
            pl.BlockSpec((None, 1, tn), lambda i, j: (row(i), 0, gate_chunk * per_chunk + j)),
        ],
        out_specs=pl.BlockSpec((tm, tn), lambda i, j: (i, j)),
        out_shape=jax.ShapeDtypeStruct((m, d), F32),
        compiler_params=_cparams(2),
        name="proj_gated_residual",
    )(a, w, res, mod3)


def _qup_kernel(qd_ref, g_ref, cos_ref, sin_ref, w_ref, o_ref, h_ref, *, heads_per_tile):
    @pl.when(pl.program_id(1) == 0)
    def _():
        def chunk(rows):
            h_ref[rows, :] = _rms(qd_ref[rows, :].astype(F32), g_ref[...]).astype(BF16)

        _by_row_chunks(qd_ref.shape[0], chunk)

    acc = _dot(h_ref[...], w_ref[...]) * Q_SCALE
    cos = cos_ref[...]
    sin = sin_ref[...]
    for hh in range(heads_per_tile):
        c0 = hh * HEAD_W
        o_ref[:, c0:c0 + LANES] = acc[:, c0:c0 + LANES].astype(o_ref.dtype)
        pe = acc[:, c0 + LANES:c0 + HEAD_W]
        o_ref[:, c0 + LANES:c0 + HEAD_W] = _rope128(pe, cos, sin).astype(o_ref.dtype)


def _qup(p, g, cos, sin, w, layer, *, qd_off):
    m = p.shape[0]
    _, kq, n = w.shape
    tm = min(1024, m)
    tn = 2048
    return pl.pallas_call(
        functools.partial(_qup_kernel, heads_per_tile=tn // HEAD_W),
        grid=(m // tm, n // tn),
        in_specs=[
            pl.BlockSpec((tm, kq), lambda i, j: (i, _blk(qd_off, kq))),
            pl.BlockSpec((1, kq), lambda i, j: (0, 0)),
            pl.BlockSpec((tm, LANES), lambda i, j: (i, 0)),
            pl.BlockSpec((tm, LANES), lambda i, j: (i, 0)),
            _wspec(w, layer, tn),
        ],
        out_specs=pl.BlockSpec((tm, tn), lambda i, j: (i, j)),
        out_shape=jax.ShapeDtypeStruct((m, n), BF16),
        scratch_shapes=[pltpu.VMEM((tm, kq), BF16)],
        compiler_params=_cparams(2),
        name="mla_q_up",
    )(p, g.reshape(1, kq), cos, sin, w)


def _kvup_kernel(kvd_ref, g_ref, kpe_ref, cos_ref, sin_ref, w_ref, o_ref, okpe_ref, h_ref):
    @pl.when(pl.program_id(1) == 0)
    def _():
        def chunk(rows):
            h_ref[rows, :] = _rms(kvd_ref[rows, :].astype(F32), g_ref[...]).astype(BF16)
            pe = kpe_ref[rows, :].astype(F32)
            okpe_ref[rows, :] = _rope128(pe, cos_ref[rows, :], sin_ref[rows, :]).astype(okpe_ref.dtype)

        _by_row_chunks(kvd_ref.shape[0], chunk)

    o_ref[...] = _dot(h_ref[...], w_ref[...]).astype(o_ref.dtype)


def _kvup(p, g, cos, sin, w, layer, *, kvd_off, kpe_off):
    m = p.shape[0]
    _, kk, n = w.shape
    tm = min(1024, m)
    tn = 2048
    return pl.pallas_call(
        _kvup_kernel,
        grid=(m // tm, n // tn),
        in_specs=[
            pl.BlockSpec((tm, kk), lambda i, j: (i, _blk(kvd_off, kk))),
            pl.BlockSpec((1, kk), lambda i, j: (0, 0)),
            pl.BlockSpec((tm, LANES), lambda i, j: (i, _blk(kpe_off, LANES))),
            pl.BlockSpec((tm, LANES), lambda i, j: (i, 0)),
            pl.BlockSpec((tm, LANES), lambda i, j: (i, 0)),
            _wspec(w, layer, tn),
        ],
        out_specs=[
            pl.BlockSpec((tm, tn), lambda i, j: (i, j)),
            pl.BlockSpec((tm, LANES), lambda i, j: (i, 0)),
        ],
        out_shape=[
            jax.ShapeDtypeStruct((m, n), BF16),
            jax.ShapeDtypeStruct((m, LANES), BF16),
        ],
        scratch_shapes=[pltpu.VMEM((tm, kk), BF16)],
        compiler_params=_cparams(2),
        name="mla_kv_up",
    )(p, g.reshape(1, kk), p, cos, sin, w)


ATTN_TK = HEAD_W
ATTN_SUB = 256
ATTN_TQ = 1024


def _attn_kernel(*refs, n_lat, n_ctx):
    if n_lat:
        q_ref, kl_ref, kpl_ref, vl_ref, kc_ref, kpc_ref, vc_ref, o_ref, kt_scr, s_scr = refs
        segments = ((kl_ref, kpl_ref, vl_ref, n_lat), (kc_ref, kpc_ref, vc_ref, n_ctx))
    else:
        q_ref, kc_ref, kpc_ref, vc_ref, o_ref, kt_scr, s_scr = refs
        segments = ((kc_ref, kpc_ref, vc_ref, n_ctx),)
    tk = ATTN_TK
    chunks = []
    base = 0
    for k_ref, kp_ref, v_ref, n in segments:
        assert n % tk == 0
        chunks += [(base + r, k_ref, kp_ref, v_ref, r) for r in range(0, n, tk)]
        base += n

    @pl.when(pl.program_id(2) == 0)
    def _():
        for off, k_ref, kp_ref, _, r in chunks:
            kt_scr[0:LANES, off:off + tk] = k_ref[r:r + tk, :].T
            kt_scr[LANES:HEAD_W, off:off + tk] = kp_ref[r:r + tk, :].T

    tq = q_ref.shape[0]
    sub = min(ATTN_SUB, tq)
    for r0 in range(0, tq, sub):
        rows = slice(r0, r0 + sub)
        q = q_ref[rows, :]
        m128 = None
        for off, *_ in chunks:
            s = _dot(q, kt_scr[:, off:off + tk])
            s_scr[rows, off:off + tk] = s
            mx = jnp.maximum(s[:, :LANES], s[:, LANES:])
            m128 = mx if m128 is None else jnp.maximum(m128, mx)
        mb = jnp.broadcast_to(jnp.max(m128, axis=-1, keepdims=True), (sub, LANES))

        l128 = jnp.zeros((sub, LANES), F32)
        acc = jnp.zeros((sub, V_DIM), F32)
        for off, _, _, v_ref, r in chunks:
            p0 = jnp.exp2(s_scr[rows, off:off + LANES] - mb)
            p1 = jnp.exp2(s_scr[rows, off + LANES:off + tk] - mb)
            l128 = l128 + (p0 + p1)
            p = jnp.concatenate([p0, p1], axis=1).astype(BF16)
            acc = acc + _dot(p, v_ref[r:r + tk, :])
        l = jnp.sum(l128, axis=-1, keepdims=True)
        o_ref[rows, :] = (acc / l).astype(o_ref.dtype)


def _attention(q, kv_c, kpe_c, *, n_batch, n_ctx, q_rows_per_batch, lat=None):
    m = q.shape[0]
    nq_rows = q_rows_per_batch
    tq = min(ATTN_TQ, nq_rows)
    nq = nq_rows // tq

    def q_map(b, h, i):
        return b * nq + i, h

    def k_map(b, h, i):
        return b, h

    def kpe_map(b, h, i):
        return b, 0

    def v_map(b, h, i):
        return b, N_HEADS + h

    n_lat = 0
    in_specs = [pl.BlockSpec((tq, HEAD_W), q_map)]
    args = [q]
    if lat is not None:
        kv_l, kpe_l = lat
        n_lat = kv_l.shape[0] // n_batch
        in_specs += [
            pl.BlockSpec((n_lat, LANES), k_map),
            pl.BlockSpec((n_lat, LANES), kpe_map),
            pl.BlockSpec((n_lat, LANES), v_map),
        ]
        args += [kv_l, kpe_l, kv_l]
    in_specs += [
        pl.BlockSpec((n_ctx, LANES), k_map),
        pl.BlockSpec((n_ctx, LANES), kpe_map),
        pl.BlockSpec((n_ctx, LANES), v_map),
    ]
    args += [kv_c, kpe_c, kv_c]
    n_keys = n_lat + n_ctx
    return pl.pallas_call(
        functools.partial(_attn_kernel, n_lat=n_lat, n_ctx=n_ctx),
        grid=(n_batch, N_HEADS, nq),
        in_specs=in_specs,
        out_specs=pl.BlockSpec((tq, V_DIM), q_map),
        out_shape=jax.ShapeDtypeStruct((m, N_HEADS * V_DIM), BF16),
        scratch_shapes=[
            pltpu.VMEM((HEAD_W, n_keys), BF16),
            pltpu.VMEM((tq, n_keys), F32),
        ],
        compiler_params=_cparams(3),
        name="mla_attention",
    )(*args)


def _chan_dft_kernel(u_ref, cs_ref, o_ref):
    cs = cs_ref[...]
    for grp in range(F_GROUPS):
        c0 = grp * F_GROUP_DIM
        r = _dot(u_ref[:, c0:c0 + F_GROUP_DIM], cs)
        o_ref[0, :, c0:c0 + F_GROUP_DIM] = r[:, :F_GROUP_DIM].astype(o_ref.dtype)
        o_ref[1, :, c0:c0 + F_GROUP_DIM] = r[:, F_GROUP_DIM:].astype(o_ref.dtype)


def _mm_kernel(a_ref, b_ref, o_ref):
    o_ref[...] = _dot(a_ref[...], b_ref[...]).astype(o_ref.dtype)


def _half_dft_kernel(c_ref, s_ref, sel_ref, a_ref, b_ref, od_ref, om_ref):
    p = _dot(c_ref[...], a_ref[...])
    q = _dot(s_ref[...], b_ref[...])
    th = od_ref.shape[0]
    od_ref[...] = (p - q)[:th].astype(od_ref.dtype)
    om_ref[...] = _dot(sel_ref[...], (p + q).astype(BF16)).astype(om_ref.dtype)


def _channel_dft(p, cs, *, n_batch, f_off):
    m = p.shape[0]
    n = m // n_batch
    fw = F_GROUPS * F_GROUP_DIM
    tm = min(512, n)
    nt = n // tm
    return pl.pallas_call(
        _chan_dft_kernel,
        grid=(n_batch, nt),
        in_specs=[
            pl.BlockSpec((tm, fw), lambda b, t: (b * nt + t, _blk(f_off, fw))),
            pl.BlockSpec((F_GROUP_DIM, 2 * F_GROUP_DIM), lambda b, t: (0, 0)),
        ],
        out_specs=pl.BlockSpec((None, 2, tm, fw), lambda b, t: (b, 0, t, 0)),
        out_shape=jax.ShapeDtypeStruct((n_batch, 2, n, fw), BF16),
        compiler_params=_cparams(2),
        name="fourier_channel_dft",
    )(p, cs)


def _fourier_halves(p, cs, tables, *, n_batch, f_off):
    c_ext, s_ext, sel = tables
    nh, ext, n = c_ext.shape
    th = sel.shape[0]
    fw = F_GROUPS * F_GROUP_DIM
    z = _channel_dft(p, cs, n_batch=n_batch, f_off=f_off)
    tn = 512
    half = jax.ShapeDtypeStruct((n_batch * n // 2, fw), BF16)
    yd, ym = pl.pallas_call(
        _half_dft_kernel,
        grid=(nh, n_batch, fw // tn),
        in_specs=[
            pl.BlockSpec((None, ext, n), lambda j, b, c: (j, 0, 0)),
            pl.BlockSpec((None, ext, n), lambda j, b, c: (j, 0, 0)),
            pl.BlockSpec((th, ext), lambda j, b, c: (0, 0)),
            pl.BlockSpec((None, None, n, tn), lambda j, b, c: (b, 0, 0, c)),
            pl.BlockSpec((None, None, n, tn), lambda j, b, c: (b, 1, 0, c)),
        ],
        out_specs=[
            pl.BlockSpec((th, tn), lambda j, b, c: (b * nh + j, c)),
            pl.BlockSpec((th, tn), lambda j, b, c: (b * nh + (nh - 1 - j), c)),
        ],
        out_shape=[half, half],
        compiler_params=_cparams(3),
        name="fourier_position_half_dft",
    )(c_ext, s_ext, sel, z, z)
    return yd, ym


def _fourier(p, cs, dft, *, n_batch, f_off):
    m = p.shape[0]
    n = m // n_batch
    fw = F_GROUPS * F_GROUP_DIM
    tm = min(512, n)
    nt = n // tm
    z = _channel_dft(p, cs, n_batch=n_batch, f_off=f_off).reshape(n_batch, 2 * n, fw)
    tn = 512
    return pl.pallas_call(
        _mm_kernel,
        grid=(nt, n_batch, fw // tn),
        in_specs=[
            pl.BlockSpec((tm, 2 * n), lambda i, b, j: (i, 0)),
            pl.BlockSpec((None, 2 * n, tn), lambda i, b, j: (b, 0, j)),
        ],
        out_specs=pl.BlockSpec((tm, tn), lambda i, b, j: (b * nt + i, j)),
        out_shape=jax.ShapeDtypeStruct((m, fw), BF16),
        compiler_params=_cparams(3),
        name="fourier_position_dft",
    )(dft, z)


CONV_ROWS = 64


def _conv_kernel(a_ref, g_ref, ap_ref, gp_ref, an_ref, gn_ref, wb_ref, cb_ref, lg_ref, lb_ref,
                 o_ref, buf, shifted, conv_scr, *, ts):
    t = pl.program_id(1)
    last = pl.num_programs(1) - 1

    def glu(a, g):
        return a.astype(F32) * jax.nn.sigmoid(g.astype(F32))

    zeros = jnp.zeros((HALO, buf.shape[1]), F32)
    buf[0:HALO, :] = jnp.where(t > 0, glu(ap_ref[...], gp_ref[...]), zeros)
    buf[HALO:HALO + ts, :] = glu(a_ref[...], g_ref[...])
    buf[HALO + ts:2 * HALO + ts, :] = jnp.where(t < last, glu(an_ref[...], gn_ref[...]), zeros)

    span = ts + 3 * SUBLANES
    for s in range(1, SUBLANES):
        shifted[s - 1, 0:span, :] = buf[s:s + span, :]

    n_lane_chunks = buf.shape[1] // LANES
    groups = CONV_ROWS // SUBLANES

    def body(rc, carry):
        r0 = pl.multiple_of(rc * CONV_ROWS, CONV_ROWS)
        for lc in range(n_lane_chunks):
            ls = slice(lc * LANES, (lc + 1) * LANES)
            accs = [jnp.zeros((SUBLANES, LANES), F32) for _ in range(groups)]
            for k in range(CONV_K):
                whole, part = divmod(k + HALO - CONV_K // 2, SUBLANES)
                w = wb_ref[k * SUBLANES:(k + 1) * SUBLANES, ls]
                for j in range(groups):
                    row = r0 + (whole + j) * SUBLANES
                    if part == 0:
                        x = buf[pl.ds(row, SUBLANES), ls]
                    else:
                        x = shifted[part - 1, pl.ds(row, SUBLANES), ls]
                    accs[j] = accs[j] + w * x
            bias = cb_ref[:, ls]
            for j in range(groups):
                conv_scr[pl.ds(r0 + j * SUBLANES, SUBLANES), ls] = accs[j] + bias
        return carry

    lax.fori_loop(0, ts // CONV_ROWS, body, 0)

    v = conv_scr[...]
    mu = jnp.mean(v, axis=-1, keepdims=True)
    dv = v - mu
    var = jnp.mean(dv * dv, axis=-1, keepdims=True)
    y = dv * lax.rsqrt(var + EPS) * lg_ref[...] + lb_ref[...]
    o_ref[...] = _silu(y).astype(o_ref.dtype)


def _conv(p, wb, cb, lg, lb, *, n_batch, a_off, g_off):
    m = p.shape[0]
    cw = cb.shape[0]
    n = m // n_batch
    ts = min(256, n)
    nt = n // ts
    halo_per_tile = ts // HALO
    last_halo = m // HALO - 1

    def main(off):
        return pl.BlockSpec((ts, cw), lambda b, t: (b * nt + t, _blk(off, cw)))

    def prev(off):
        return pl.BlockSpec(
            (HALO, cw), lambda b, t: (jnp.maximum((b * nt + t) * halo_per_tile - 1, 0), _blk(off, cw)))

    def nxt(off):
        return pl.BlockSpec(
            (HALO, cw),
            lambda b, t: (jnp.minimum((b * nt + t + 1) * halo_per_tile, last_halo), _blk(off, cw)))

    vec = pl.BlockSpec((1, cw), lambda b, t: (0, 0))
    return pl.pallas_call(
        functools.partial(_conv_kernel, ts=ts),
        grid=(n_batch, nt),
        in_specs=[
            main(a_off), main(g_off), prev(a_off), prev(g_off), nxt(a_off), nxt(g_off),
            pl.BlockSpec((CONV_K * SUBLANES, cw), lambda b, t: (0, 0)),
            vec, vec, vec,
        ],
        out_specs=pl.BlockSpec((ts, cw), lambda b, t: (b * nt + t, 0)),
        out_shape=jax.ShapeDtypeStruct((m, cw), BF16),
        scratch_shapes=[
            pltpu.VMEM((ts + 2 * HALO, cw), F32),
            pltpu.VMEM((SUBLANES - 1, ts + 3 * SUBLANES, cw), F32),
            pltpu.VMEM((ts, cw), F32),
        ],
        compiler_params=_cparams(2),
        name="conformer_conv",
    )(p, p, p, p, p, p, wb, cb.reshape(1, cw), lg.reshape(1, cw), lb.reshape(1, cw))


def _merge_kernel(yd_ref, ym_ref, cv_ref, oa_ref, gf_ref, gc_ref, ga_ref, bf_ref, bc_ref, ba_ref,
                  wf_ref, wc_ref, wa_ref, o_ref, *, tiles_per_batch):
    def gate(g_ref, b_ref):
        return jax.nn.sigmoid(g_ref[...].astype(F32) + b_ref[...])

    if tiles_per_batch:
        first = (pl.program_id(0) % tiles_per_batch) < tiles_per_batch // 2
        yf = jnp.where(first, yd_ref[...], ym_ref[...])
    else:
        yf = yd_ref[...]
    acc = gate(gf_ref, bf_ref) * _dot(yf, wf_ref[...])
    acc = acc + gate(gc_ref, bc_ref) * _dot(cv_ref[...], wc_ref[...])
    acc = acc + gate(ga_ref, ba_ref) * _dot(oa_ref[...], wa_ref[...])
    o_ref[...] = acc.astype(o_ref.dtype)


def _merge(yf, cv, oa, p, b_gate, wf, wc, wa, layer, *, gate_off, rows_per_batch=None):
    m = cv.shape[0]
    d = wf.shape[2]
    tn = 512
    nj = d // tn
    g0 = _blk(gate_off, tn)
    if isinstance(yf, tuple):
        yd, ym = yf
        tm = min(1024, rows_per_batch // 2)
        tiles_per_batch = rows_per_batch // tm
        hpb = tiles_per_batch // 2

        def half_tile(i):
            return (i // tiles_per_batch) * hpb, i % tiles_per_batch

        def yd_map(i, j):
            base, it = half_tile(i)
            return base + jnp.minimum(it, hpb - 1), 0

        def ym_map(i, j):
            base, it = half_tile(i)
            return base + jnp.maximum(it - hpb, 0), 0

        y_specs = [pl.BlockSpec((tm, yd.shape[1]), yd_map), pl.BlockSpec((tm, ym.shape[1]), ym_map)]
    else:
        yd = ym = yf
        tm = min(1024, m)
        tiles_per_batch = 0
        y_specs = [pl.BlockSpec((tm, yf.shape[1]), lambda i, j: (i, 0))] * 2

    def act(a):
        return pl.BlockSpec((tm, a.shape[1]), lambda i, j: (i, 0))

    def gate(br):
        return pl.BlockSpec((tm, tn), lambda i, j: (i, g0 + br * nj + j))

    def bias(br):
        return pl.BlockSpec((None, 1, tn), lambda i, j: (layer, 0, br * nj + j))

    def weight(w):
        return _wspec(w, layer, tn)

    return pl.pallas_call(
        functools.partial(_merge_kernel, tiles_per_batch=tiles_per_batch),
        grid=(m // tm, nj),
        in_specs=y_specs + [act(cv), act(oa), gate(0), gate(1), gate(2),
                            bias(0), bias(1), bias(2), weight(wf), weight(wc), weight(wa)],
        out_specs=pl.BlockSpec((tm, tn), lambda i, j: (i, j)),
        out_shape=jax.ShapeDtypeStruct((m, d), BF16),
        compiler_params=_cparams(2),
        name="branch_merge",
    )(yd, ym, cv, oa, p, p, p, b_gate, b_gate, b_gate, wf, wc, wa)


def _final_norm_kernel(x_ref, g_ref, o_ref):
    def chunk(rows):
        o_ref[rows, :] = _rms(x_ref[rows, :], g_ref[...])

    _by_row_chunks(x_ref.shape[0], chunk)


def _final_norm(x, g):
    m, d = x.shape
    tm = min(512, m)
    return pl.pallas_call(
        _final_norm_kernel,
        grid=(m // tm,),
        in_specs=[pl.BlockSpec((tm, d), lambda i: (i, 0)), pl.BlockSpec((1, d), lambda i: (0, 0))],
        out_specs=pl.BlockSpec((tm, d), lambda i: (i, 0)),
        out_shape=jax.ShapeDtypeStruct((m, d), F32),
        compiler_params=_cparams(1),
        name="final_rmsnorm",
    )(x, g.reshape(1, d))


def _rope_tables(n_tok):
    rows = n_tok // GRID_W
    row = jnp.broadcast_to(jnp.arange(rows, dtype=F32)[:, None], (rows, GRID_W)).reshape(-1)
    col = jnp.broadcast_to(jnp.arange(GRID_W, dtype=F32)[None, :], (rows, GRID_W)).reshape(-1)
    n_freq = QK_ROPE // 4
    inv_freq = ROPE_BASE ** (-jnp.arange(n_freq, dtype=F32) / n_freq)
    ang = jnp.concatenate([row[:, None] * inv_freq, col[:, None] * inv_freq], axis=-1)
    pad = jnp.zeros((n_tok, LANES - QK_ROPE), F32)
    cos = jnp.concatenate([jnp.cos(ang), jnp.cos(ang), pad], axis=-1)
    sin = jnp.concatenate([jnp.sin(ang), jnp.sin(ang), pad], axis=-1)
    return cos, sin


def _dft_cos_sin(n, k=None):
    n2 = min(n, GRID_W)
    n1 = n // n2
    k = (jnp.arange(n, dtype=jnp.int32) if k is None else k)[:, None]
    rows = k.shape[0]
    ang1 = (((k % n1) * jnp.arange(n1, dtype=jnp.int32)[None, :]) % n1).astype(F32) * (2.0 * jnp.pi / n1)
    ang2 = ((k * jnp.arange(n2, dtype=jnp.int32)[None, :]) % n).astype(F32) * (2.0 * jnp.pi / n)
    scale = n ** -0.5
    c1, s1 = jnp.cos(ang1)[:, :, None] * scale, jnp.sin(ang1)[:, :, None] * scale
    c2, s2 = jnp.cos(ang2)[:, None, :], jnp.sin(ang2)[:, None, :]
    return (c1 * c2 - s1 * s2).reshape(rows, n), (s1 * c2 + c1 * s2).reshape(rows, n)


DFT_HALF_TILE = 512


def _half_dft_tables(n):
    th = min(DFT_HALF_TILE, n // 2)
    nh = n // 2 // th
    ext = th + BF16_ROWS
    k = (jnp.arange(nh, dtype=jnp.int32)[:, None] * th + jnp.arange(ext, dtype=jnp.int32)[None, :])
    c, s = _dft_cos_sin(n, k.reshape(-1))
    sel = jnp.arange(ext, dtype=jnp.int32)[None, :] == th - jnp.arange(th, dtype=jnp.int32)[:, None]
    return (c.reshape(nh, ext, n).astype(BF16), s.reshape(nh, ext, n).astype(BF16), sel.astype(BF16))


def kernel(x, c, ctx, c_ctx, ada_w, ada_b, norm_mix_g, w_in, b_gate, w_fourier, conv_w, conv_b,
           conv_ln_g, conv_ln_b, w_conv_out, q_norm_g, w_uq, kv_norm_g, w_ukv, w_mla_o, w_out,
           norm_ffn_g, w_ffn_gate, w_ffn_up, w_ffn_down, final_norm_g):
    n_batch, seq, d = x.shape
    n_ctx = ctx.shape[1]
    depth = ada_w.shape[0]
    assert n_batch < SUBLANES
    q_lora = q_norm_g.shape[1]
    kv_lora = kv_norm_g.shape[1]
    f_width = w_fourier.shape[1]
    c_width = conv_w.shape[2]
    assert (f_width, c_width, q_lora, kv_lora) == (PX_A - PX_F, PX_G - PX_A, PX_GATE - PX_QD,
                                                    PX_KPE - PX_KVD)
    f_end = f_width
    c_end = f_end + 2 * c_width
    q_end = c_end + q_lora
    kv_end = q_end + kv_lora
    kpe_end = kv_end + QK_ROPE

    zeros = lambda k: jnp.zeros((depth, d, k), w_in.dtype)
    w_in_r = jnp.concatenate([
        w_in[:, :, :c_end],
        w_in[:, :, q_end:kv_end],
        w_in[:, :, kv_end:kpe_end], zeros(LANES - QK_ROPE),
        zeros(PX_QD - PX_KPE - LANES),
        w_in[:, :, c_end:q_end],
        w_in[:, :, kpe_end:],
    ], axis=-1).astype(BF16)
    assert w_in_r.shape[-1] == PX_COLS
    w_uq_h = w_uq.reshape(depth, q_lora, N_HEADS, QK_NOPE + QK_ROPE)
    w_uq_r = jnp.concatenate(
        [w_uq_h, jnp.zeros((depth, q_lora, N_HEADS, HEAD_W - QK_NOPE - QK_ROPE), w_uq.dtype)],
        axis=-1).reshape(depth, q_lora, N_HEADS * HEAD_W).astype(BF16)
    w_ukv_r = w_ukv.reshape(depth, kv_lora, N_HEADS, 2, QK_NOPE).transpose(0, 1, 3, 2, 4).reshape(
        depth, kv_lora, 2 * N_HEADS * QK_NOPE).astype(BF16)
    w_fourier_b = w_fourier.astype(BF16)
    w_conv_out_b = w_conv_out.astype(BF16)
    w_mla_o_b = w_mla_o.astype(BF16)
    w_out_b = w_out.astype(BF16)
    w_gate_b = w_ffn_gate.astype(BF16)
    w_up_b = w_ffn_up.astype(BF16)
    w_down_b = w_ffn_down.astype(BF16)
    conv_wb = jnp.repeat(conv_w, SUBLANES, axis=1)
    b_gate3 = b_gate.reshape(depth, 1, -1)

    cos_l, sin_l = _rope_tables(seq)
    cos_l = jnp.tile(cos_l, (n_batch, 1))
    sin_l = jnp.tile(sin_l, (n_batch, 1))
    cos_c = jnp.ones((n_batch * n_ctx, LANES), F32)
    sin_c = jnp.zeros((n_batch * n_ctx, LANES), F32)
    c128, s128 = _dft_cos_sin(F_GROUP_DIM)
    cs = jnp.concatenate([c128, s128], axis=1).astype(BF16)
    half_dft_l = _half_dft_tables(seq)
    cc, sc = _dft_cos_sin(n_ctx)
    dft_c = jnp.concatenate([cc, -sc], axis=1).astype(BF16)

    c8 = jnp.concatenate(
        [c, c_ctx[None, :], jnp.zeros((SUBLANES - n_batch - 1, d), c.dtype)], axis=0)
    mod_all = _ada(c8, ada_w, ada_b).reshape(depth, SUBLANES, 1, 6 * d)

    lat = _Rows(n_batch * seq, seq, None)
    cxt = _Rows(n_batch * n_ctx, None, n_batch)
    xs = x.reshape(n_batch * seq, d)
    cs_rows = ctx.reshape(n_batch * n_ctx, d)

    def mix_and_ffn(i, rows, h_in, p, o_attn, nb):
        mod3 = mod_all[i]
        if rows is lat:
            yf = _fourier_halves(p, cs, half_dft_l, n_batch=nb, f_off=PX_F)
        else:
            yf = _fourier(p, cs, dft_c, n_batch=nb, f_off=PX_F)
        cv = _conv(p, conv_wb[i], conv_b[i], conv_ln_g[i], conv_ln_b[i],
                   n_batch=nb, a_off=PX_A, g_off=PX_G)
        merged = _merge(yf, cv, o_attn, p, b_gate3, w_fourier_b, w_conv_out_b, w_mla_o_b, i,
                        gate_off=PX_GATE, rows_per_batch=rows.rows_per_batch)
        mixed = _resmm(merged, w_out_b, i, h_in, mod3, rows, gate_chunk=2, tm=2048, tn=512)
        hid = _ffn_up(mixed, norm_ffn_g[i], mod3, w_gate_b, w_up_b, i, rows)
        return _resmm(hid, w_down_b, i, mixed, mod3, rows, gate_chunk=5, tm=1024, tn=512)

    for i in range(depth):
        last = i == depth - 1
        mod3 = mod_all[i]
        px = _inproj(xs, norm_mix_g[i], mod3, w_in_r, i, lat, shift_chunk=0, tn=1536)
        if last:
            pc = _inproj(cs_rows, norm_mix_g[i], mod3, w_in_r, i, cxt, shift_chunk=0,
                         tn=PKV_COLS, col_off=PX_KVD, n=PKV_COLS)
            kvd_off, kpe_off = PKV_KVD, PKV_KPE
        else:
            pc = _inproj(cs_rows, norm_mix_g[i], mod3, w_in_r, i, cxt, shift_chunk=0, tn=1536)
            kvd_off, kpe_off = PX_KVD, PX_KPE

        q_x = _qup(px, q_norm_g[i], cos_l, sin_l, w_uq_r, i, qd_off=PX_QD)
        kv_x, kpe_x = _kvup(px, kv_norm_g[i], cos_l, sin_l, w_ukv_r, i,
                            kvd_off=PX_KVD, kpe_off=PX_KPE)
        kv_c, kpe_c = _kvup(pc, kv_norm_g[i], cos_c, sin_c, w_ukv_r, i,
                            kvd_off=kvd_off, kpe_off=kpe_off)
        o_x = _attention(q_x, kv_c, kpe_c, n_batch=n_batch, n_ctx=n_ctx,
                         q_rows_per_batch=seq, lat=(kv_x, kpe_x))
        if not last:
            q_c = _qup(pc, q_norm_g[i], cos_c, sin_c, w_uq_r, i, qd_off=PX_QD)
            o_c = _attention(q_c, kv_c, kpe_c, n_batch=n_batch, n_ctx=n_ctx,
                             q_rows_per_batch=n_ctx)
            cs_rows = mix_and_ffn(i, cxt, cs_rows, pc, o_c, n_batch)
        xs = mix_and_ffn(i, lat, xs, px, o_x, n_batch)

    return _final_norm(xs, final_norm_g).reshape(n_batch, seq, d)
```

```python
import functools

import numpy as np
import jax
import jax.numpy as jnp
from jax import lax
from jax.experimental import pallas as pl
from jax.experimental.pallas import tpu as pltpu

F32 = jnp.float32
BF16 = jnp.bfloat16

GRID_W = 64
N_HEADS = 16
QK_NOPE = 128
QK_ROPE = 64
V_DIM = 128
F_GROUPS = 8
F_GROUP_DIM = 128
CONV_K = 31
ROPE_BASE = 10000.0
ATTN_SCALE = (QK_NOPE + QK_ROPE) ** -0.5
Q_SCALE = ATTN_SCALE * 1.4426950408889634
EPS = 1e-6

LANES = 128
SUBLANES = 8
BF16_ROWS = 16
VMEM_LIMIT = 56 * 1024 * 1024

HEAD_W = 2 * LANES
HALO = BF16_ROWS

PX_F = 0
PX_A = 1024
PX_G = 2048
PX_KVD = 3072
PX_KPE = 3584
PX_QD = 3840
PX_GATE = 4608
PX_COLS = 10752
PKV_KVD = 0
PKV_KPE = 512
PKV_COLS = 768


def _cparams(n_axes):
    return pltpu.CompilerParams(
        dimension_semantics=("arbitrary",) * n_axes, vmem_limit_bytes=VMEM_LIMIT)


def _blk(off, width):
    assert off % width == 0, (off, width)
    return off // width


def _dot(a, b):
    return jnp.dot(a, b, preferred_element_type=F32)


def _silu(v):
    return v * jax.nn.sigmoid(v)


def _rms(x, g):
    r = lax.rsqrt(jnp.mean(x * x, axis=-1, keepdims=True) + EPS)
    return (x * r) * g


ROW_CHUNK_UNROLL = 8


def _by_row_chunks(n_rows, fn):
    def body(c, carry):
        fn(pl.ds(pl.multiple_of(c * BF16_ROWS, BF16_ROWS), BF16_ROWS))
        return carry

    lax.fori_loop(0, n_rows // BF16_ROWS, body, 0, unroll=ROW_CHUNK_UNROLL)


def _rope128(pe, cos, sin):
    lane = lax.broadcasted_iota(jnp.int32, pe.shape, 1)
    half = QK_ROPE // 2
    rot = jnp.where(lane < half, -pltpu.roll(pe, LANES - half, 1), pltpu.roll(pe, half, 1))
    return pe * cos + rot * sin


def _ada_kernel(c_ref, w_ref, b_ref, o_ref):
    a = _silu(c_ref[...]).astype(BF16)
    o_ref[...] = _dot(a, w_ref[...].astype(BF16)) + b_ref[...]


def _ada(c8, ada_w, ada_b):
    depth, d, n6 = ada_w.shape
    tn = 1024
    return pl.pallas_call(
        _ada_kernel,
        grid=(depth, n6 // tn),
        in_specs=[
            pl.BlockSpec((SUBLANES, d), lambda l, j: (0, 0)),
            pl.BlockSpec((None, d, tn), lambda l, j: (l, 0, j)),
            pl.BlockSpec((None, 1, tn), lambda l, j: (l, 0, j)),
        ],
        out_specs=pl.BlockSpec((None, SUBLANES, tn), lambda l, j: (l, 0, j)),
        out_shape=jax.ShapeDtypeStruct((depth, SUBLANES, n6), F32),
        compiler_params=_cparams(2),
        name="ada_mod",
    )(c8, ada_w, ada_b.reshape(depth, 1, n6))


class _Rows:
    def __init__(self, m, rows_per_batch, fixed_mod_row):
        self.m = m
        self.rows_per_batch = rows_per_batch
        self.fixed_mod_row = fixed_mod_row

    def tile(self, preferred):
        return min(preferred, self.rows_per_batch or self.m)

    def mod_row(self, tm):
        if self.fixed_mod_row is not None:
            row = self.fixed_mod_row
            return lambda i: row
        rpb = self.rows_per_batch
        assert rpb % tm == 0
        return lambda i: (i * tm) // rpb


def _mod_spec(rows, tm, d, chunk):
    row = rows.mod_row(tm)
    return pl.BlockSpec((None, 1, d), lambda i, j: (row(i), 0, chunk))


def _norm_modulate(x_ref, g_ref, sc_ref, sh_ref, h_ref):
    def chunk(rows):
        h = _rms(x_ref[rows, :], g_ref[...]) * (1.0 + sc_ref[...]) + sh_ref[...]
        h_ref[rows, :] = h.astype(h_ref.dtype)

    _by_row_chunks(x_ref.shape[0], chunk)


def _inproj_kernel(x_ref, g_ref, sc_ref, sh_ref, w_ref, o_ref, h_ref):
    @pl.when(pl.program_id(1) == 0)
    def _():
        _norm_modulate(x_ref, g_ref, sc_ref, sh_ref, h_ref)

    o_ref[...] = _dot(h_ref[...], w_ref[...]).astype(o_ref.dtype)


def _wspec(w, layer, tn, col0=0):
    return pl.BlockSpec((None, w.shape[1], tn), lambda i, j: (layer, 0, col0 + j))


def _inproj(x, g, mod3, w, layer, rows, *, shift_chunk, tn, col_off=0, n=None):
    m, d = x.shape
    n = n or w.shape[2]
    tm = rows.tile(1024)
    return pl.pallas_call(
        _inproj_kernel,
        grid=(m // tm, n // tn),
        in_specs=[
            pl.BlockSpec((tm, d), lambda i, j: (i, 0)),
            pl.BlockSpec((1, d), lambda i, j: (0, 0)),
            _mod_spec(rows, tm, d, shift_chunk + 1),
            _mod_spec(rows, tm, d, shift_chunk),
            _wspec(w, layer, tn, _blk(col_off, tn)),
        ],
        out_specs=pl.BlockSpec((tm, tn), lambda i, j: (i, j)),
        out_shape=jax.ShapeDtypeStruct((m, n), BF16),
        scratch_shapes=[pltpu.VMEM((tm, d), BF16)],
        compiler_params=_cparams(2),
        name="norm_mod_proj",
    )(x, g.reshape(1, d), mod3, mod3, w)


def _ffn_up_kernel(x_ref, g_ref, sc_ref, sh_ref, wg_ref, wu_ref, o_ref, h_ref):
    @pl.when(pl.program_id(1) == 0)
    def _():
        _norm_modulate(x_ref, g_ref, sc_ref, sh_ref, h_ref)

    hb = h_ref[...]
    o_ref[...] = (_silu(_dot(hb, wg_ref[...])) * _dot(hb, wu_ref[...])).astype(o_ref.dtype)


def _ffn_up(x, g, mod3, wg, wu, layer, rows):
    m, d = x.shape
    n = wg.shape[2]
    tm = rows.tile(1024)
    tn = 512
    return pl.pallas_call(
        _ffn_up_kernel,
        grid=(m // tm, n // tn),
        in_specs=[
            pl.BlockSpec((tm, d), lambda i, j: (i, 0)),
            pl.BlockSpec((1, d), lambda i, j: (0, 0)),
            _mod_spec(rows, tm, d, 4),
            _mod_spec(rows, tm, d, 3),
            _wspec(wg, layer, tn),
            _wspec(wu, layer, tn),
        ],
        out_specs=pl.BlockSpec((tm, tn), lambda i, j: (i, j)),
        out_shape=jax.ShapeDtypeStruct((m, n), BF16),
        scratch_shapes=[pltpu.VMEM((tm, d), BF16)],
        compiler_params=_cparams(2),
        name="ffn_gate_up",
    )(x, g.reshape(1, d), mod3, mod3, wg, wu)


def _resmm_kernel(a_ref, w_ref, res_ref, gate_ref, o_ref):
    o_ref[...] = res_ref[...] + gate_ref[...] * _dot(a_ref[...], w_ref[...])


def _resmm(a, w, layer, res, mod3, rows, *, gate_chunk, tm, tn):
    m, k = a.shape
    d = w.shape[2]
    tm = rows.tile(tm)
    row = rows.mod_row(tm)
    per_chunk = d // tn
    return pl.pallas_call(
        _resmm_kernel,
        grid=(m // tm, d // tn),
        in_specs=[
            pl.BlockSpec((tm, k), lambda i, j: (i, 0)),
            _wspec(w, layer, tn),
            pl.BlockSpec((tm, tn), lambda i, j: (i, j)),
            pl.BlockSpec((None, 1, tn), lambda i, j: (row(i), 0, gate_chunk * per_chunk + j)),
        ],
        out_specs=pl.BlockSpec((tm, tn), lambda i, j: (i, j)),
        out_shape=jax.ShapeDtypeStruct((m, d), F32),
        compiler_params=_cparams(2),
        name="proj_gated_residual",
    )(a, w, res, mod3)


def _qup_kernel(qd_ref, g_ref, cos_ref, sin_ref, w_ref, o_ref, h_ref, *, heads_per_tile):
    @pl.when(pl.program_id(1) == 0)
    def _():
        h_ref[...] = _rms(qd_ref[...].astype(F32), g_ref[...]).astype(BF16)

    acc = _dot(h_ref[...], w_ref[...]) * Q_SCALE
    cos = cos_ref[...]
    sin = sin_ref[...]
    for hh in range(heads_per_tile):
        c0 = hh * HEAD_W
        o_ref[:, c0:c0 + LANES] = acc[:, c0:c0 + LANES].astype(o_ref.dtype)
        pe = acc[:, c0 + LANES:c0 + HEAD_W]
        o_ref[:, c0 + LANES:c0 + HEAD_W] = _rope128(pe, cos, sin).astype(o_ref.dtype)


def _rope_spec(table, tm):
    tiles = table.shape[0] // tm
    return pl.BlockSpec((tm, LANES), lambda i, j: (i % tiles, 0))


def _qup(p, g, cos, sin, w, layer, *, qd_off):
    m = p.shape[0]
    _, kq, n = w.shape
    tm = min(1024, m, cos.shape[0])
    tn = 2048
    return pl.pallas_call(
        functools.partial(_qup_kernel, heads_per_tile=tn // HEAD_W),
        grid=(m // tm, n // tn),
        in_specs=[
            pl.BlockSpec((tm, kq), lambda i, j: (i, _blk(qd_off, kq))),
            pl.BlockSpec((1, kq), lambda i, j: (0, 0)),
            _rope_spec(cos, tm),
            _rope_spec(sin, tm),
            _wspec(w, layer, tn),
        ],
        out_specs=pl.BlockSpec((tm, tn), lambda i, j: (i, j)),
        out_shape=jax.ShapeDtypeStruct((m, n), BF16),
        scratch_shapes=[pltpu.VMEM((tm, kq), BF16)],
        compiler_params=_cparams(2),
        name="mla_q_up",
    )(p, g.reshape(1, kq), cos, sin, w)


def _kvup_kernel(kvd_ref, g_ref, kpe_ref, cos_ref, sin_ref, w_ref, o_ref, okpe_ref, h_ref):
    @pl.when(pl.program_id(1) == 0)
    def _():
        h_ref[...] = _rms(kvd_ref[...].astype(F32), g_ref[...]).astype(BF16)
        pe = kpe_ref[...].astype(F32)
        okpe_ref[...] = _rope128(pe, cos_ref[...], sin_ref[...]).astype(okpe_ref.dtype)

    o_ref[...] = _dot(h_ref[...], w_ref[...]).astype(o_ref.dtype)


def _kvup(p, g, cos, sin, w, layer, *, kvd_off, kpe_off):
    m = p.shape[0]
    _, kk, n = w.shape
    tm = min(1024, m, cos.shape[0])
    tn = 2048
    return pl.pallas_call(
        _kvup_kernel,
        grid=(m // tm, n // tn),
        in_specs=[
            pl.BlockSpec((tm, kk), lambda i, j: (i, _blk(kvd_off, kk))),
            pl.BlockSpec((1, kk), lambda i, j: (0, 0)),
            pl.BlockSpec((tm, LANES), lambda i, j: (i, _blk(kpe_off, LANES))),
            _rope_spec(cos, tm),
            _rope_spec(sin, tm),
            _wspec(w, layer, tn),
        ],
        out_specs=[
            pl.BlockSpec((tm, tn), lambda i, j: (i, j)),
            pl.BlockSpec((tm, LANES), lambda i, j: (i, 0)),
        ],
        out_shape=[
            jax.ShapeDtypeStruct((m, n), BF16),
            jax.ShapeDtypeStruct((m, LANES), BF16),
        ],
        scratch_shapes=[pltpu.VMEM((tm, kk), BF16)],
        compiler_params=_cparams(2),
        name="mla_kv_up",
    )(p, g.reshape(1, kk), p, cos, sin, w)


ATTN_TK = HEAD_W
ATTN_SUB = 256
ATTN_TQ = 1024


def _attn_kernel(*refs, n_lat, n_ctx):
    if n_lat:
        q_ref, kl_ref, kpl_ref, vl_ref, kc_ref, kpc_ref, vc_ref, o_ref, kt_scr, s_scr = refs
        segments = ((kl_ref, kpl_ref, vl_ref, n_lat), (kc_ref, kpc_ref, vc_ref, n_ctx))
    else:
        q_ref, kc_ref, kpc_ref, vc_ref, o_ref, kt_scr, s_scr = refs
        segments = ((kc_ref, kpc_ref, vc_ref, n_ctx),)
    tk = ATTN_TK
    chunks = []
    base = 0
    for k_ref, kp_ref, v_ref, n in segments:
        assert n % tk == 0
        chunks += [(base + r, k_ref, kp_ref, v_ref, r) for r in range(0, n, tk)]
        base += n

    @pl.when(pl.program_id(2) == 0)
    def _():
        for off, k_ref, kp_ref, _, r in chunks:
            kt_scr[0:LANES, off:off + tk] = k_ref[r:r + tk, :].T
            kt_scr[LANES:HEAD_W, off:off + tk] = kp_ref[r:r + tk, :].T

    tq = q_ref.shape[0]
    sub = min(ATTN_SUB, tq)
    for r0 in range(0, tq, sub):
        rows = slice(r0, r0 + sub)
        q = q_ref[rows, :]
        m128 = None
        for off, *_ in chunks:
            s = _dot(q, kt_scr[:, off:off + tk])
            s_scr[rows, off:off + tk] = s
            mx = jnp.maximum(s[:, :LANES], s[:, LANES:])
            m128 = mx if m128 is None else jnp.maximum(m128, mx)
        mb = jnp.broadcast_to(jnp.max(m128, axis=-1, keepdims=True), (sub, LANES))

        l128 = jnp.zeros((sub, LANES), F32)
        acc = jnp.zeros((sub, V_DIM), F32)
        for off, _, _, v_ref, r in chunks:
            p0 = jnp.exp2(s_scr[rows, off:off + LANES] - mb)
            p1 = jnp.exp2(s_scr[rows, off + LANES:off + tk] - mb)
            l128 = l128 + (p0 + p1)
            p = jnp.concatenate([p0, p1], axis=1).astype(BF16)
            acc = acc + _dot(p, v_ref[r:r + tk, :])
        l = jnp.sum(l128, axis=-1, keepdims=True)
        o_ref[rows, :] = (acc / l).astype(o_ref.dtype)


def _attention(q, kv_c, kpe_c, *, n_batch, n_ctx, q_rows_per_batch, lat=None):
    m = q.shape[0]
    nq_rows = q_rows_per_batch
    tq = min(ATTN_TQ, nq_rows)
    nq = nq_rows // tq

    def q_map(b, h, i):
        return b * nq + i, h

    def k_map(b, h, i):
        return b, h

    def kpe_map(b, h, i):
        return b, 0

    def v_map(b, h, i):
        return b, N_HEADS + h

    n_lat = 0
    in_specs = [pl.BlockSpec((tq, HEAD_W), q_map)]
    args = [q]
    if lat is not None:
        kv_l, kpe_l = lat
        n_lat = kv_l.shape[0] // n_batch
        in_specs += [
            pl.BlockSpec((n_lat, LANES), k_map),
            pl.BlockSpec((n_lat, LANES), kpe_map),
            pl.BlockSpec((n_lat, LANES), v_map),
        ]
        args += [kv_l, kpe_l, kv_l]
    in_specs += [
        pl.BlockSpec((n_ctx, LANES), k_map),
        pl.BlockSpec((n_ctx, LANES), kpe_map),
        pl.BlockSpec((n_ctx, LANES), v_map),
    ]
    args += [kv_c, kpe_c, kv_c]
    n_keys = n_lat + n_ctx
    return pl.pallas_call(
        functools.partial(_attn_kernel, n_lat=n_lat, n_ctx=n_ctx),
        grid=(n_batch, N_HEADS, nq),
        in_specs=in_specs,
        out_specs=pl.BlockSpec((tq, V_DIM), q_map),
        out_shape=jax.ShapeDtypeStruct((m, N_HEADS * V_DIM), BF16),
        scratch_shapes=[
            pltpu.VMEM((HEAD_W, n_keys), BF16),
            pltpu.VMEM((tq, n_keys), F32),
        ],
        compiler_params=_cparams(3),
        name="mla_attention",
    )(*args)


def _chan_dft_kernel(u_ref, cs_ref, o_ref):
    cs = cs_ref[...]
    for grp in range(F_GROUPS):
        c0 = grp * F_GROUP_DIM
        r = _dot(u_ref[:, c0:c0 + F_GROUP_DIM], cs)
        o_ref[0, :, c0:c0 + F_GROUP_DIM] = r[:, :F_GROUP_DIM].astype(o_ref.dtype)
        o_ref[1, :, c0:c0 + F_GROUP_DIM] = r[:, F_GROUP_DIM:].astype(o_ref.dtype)


def _mm_kernel(a_ref, b_ref, o_ref):
    o_ref[...] = _dot(a_ref[...], b_ref[...]).astype(o_ref.dtype)


def _half_dft_kernel(c_ref, s_ref, sel_ref, a_ref, b_ref, od_ref, om_ref):
    p = _dot(c_ref[...], a_ref[...])
    q = _dot(s_ref[...], b_ref[...])
    th = od_ref.shape[0]
    od_ref[...] = (p - q)[:th].astype(od_ref.dtype)
    om_ref[...] = _dot(sel_ref[...], (p + q).astype(BF16)).astype(om_ref.dtype)


def _channel_dft(p, cs, *, n_batch, f_off):
    m = p.shape[0]
    n = m // n_batch
    fw = F_GROUPS * F_GROUP_DIM
    tm = min(512, n)
    nt = n // tm
    return pl.pallas_call(
        _chan_dft_kernel,
        grid=(n_batch, nt),
        in_specs=[
            pl.BlockSpec((tm, fw), lambda b, t: (b * nt + t, _blk(f_off, fw))),
            pl.BlockSpec((F_GROUP_DIM, 2 * F_GROUP_DIM), lambda b, t: (0, 0)),
        ],
        out_specs=pl.BlockSpec((None, 2, tm, fw), lambda b, t: (b, 0, t, 0)),
        out_shape=jax.ShapeDtypeStruct((n_batch, 2, n, fw), BF16),
        compiler_params=_cparams(2),
        name="fourier_channel_dft",
    )(p, cs)


def _fourier_halves(p, cs, tables, *, n_batch, f_off):
    c_ext, s_ext, sel = tables
    nh, ext, n = c_ext.shape
    th = sel.shape[0]
    fw = F_GROUPS * F_GROUP_DIM
    z = _channel_dft(p, cs, n_batch=n_batch, f_off=f_off)
    tn = 512
    half = jax.ShapeDtypeStruct((n_batch * n // 2, fw), BF16)
    yd, ym = pl.pallas_call(
        _half_dft_kernel,
        grid=(nh, n_batch, fw // tn),
        in_specs=[
            pl.BlockSpec((None, ext, n), lambda j, b, c: (j, 0, 0)),
            pl.BlockSpec((None, ext, n), lambda j, b, c: (j, 0, 0)),
            pl.BlockSpec((th, ext), lambda j, b, c: (0, 0)),
            pl.BlockSpec((None, None, n, tn), lambda j, b, c: (b, 0, 0, c)),
            pl.BlockSpec((None, None, n, tn), lambda j, b, c: (b, 1, 0, c)),
        ],
        out_specs=[
            pl.BlockSpec((th, tn), lambda j, b, c: (b * nh + j, c)),
            pl.BlockSpec((th, tn), lambda j, b, c: (b * nh + (nh - 1 - j), c)),
        ],
        out_shape=[half, half],
        compiler_params=_cparams(3),
        name="fourier_position_half_dft",
    )(c_ext, s_ext, sel, z, z)
    return yd, ym


def _fourier(p, cs, dft, *, n_batch, f_off):
    m = p.shape[0]
    n = m // n_batch
    fw = F_GROUPS * F_GROUP_DIM
    tm = min(512, n)
    nt = n // tm
    z = _channel_dft(p, cs, n_batch=n_batch, f_off=f_off).reshape(n_batch, 2 * n, fw)
    tn = 512
    return pl.pallas_call(
        _mm_kernel,
        grid=(nt, n_batch, fw // tn),
        in_specs=[
            pl.BlockSpec((tm, 2 * n), lambda i, b, j: (i, 0)),
            pl.BlockSpec((None, 2 * n, tn), lambda i, b, j: (b, 0, j)),
        ],
        out_specs=pl.BlockSpec((tm, tn), lambda i, b, j: (b * nt + i, j)),
        out_shape=jax.ShapeDtypeStruct((m, fw), BF16),
        compiler_params=_cparams(3),
        name="fourier_position_dft",
    )(dft, z)


CONV_ROWS = 64


def _conv_kernel(a_ref, g_ref, ap_ref, gp_ref, an_ref, gn_ref, wb_ref, cb_ref, lg_ref, lb_ref,
                 o_ref, buf, shifted, conv_scr, *, ts):
    t = pl.program_id(1)
    last = pl.num_programs(1) - 1

    def glu(a, g):
        return a.astype(F32) * jax.nn.sigmoid(g.astype(F32))

    zeros = jnp.zeros((HALO, buf.shape[1]), F32)
    buf[0:HALO, :] = jnp.where(t > 0, glu(ap_ref[...], gp_ref[...]), zeros)
    buf[HALO:HALO + ts, :] = glu(a_ref[...], g_ref[...])
    buf[HALO + ts:2 * HALO + ts, :] = jnp.where(t < last, glu(an_ref[...], gn_ref[...]), zeros)

    span = ts + 3 * SUBLANES
    for s in range(1, SUBLANES):
        shifted[s - 1, 0:span, :] = buf[s:s + span, :]

    n_lane_chunks = buf.shape[1] // LANES
    groups = CONV_ROWS // SUBLANES

    def body(rc, carry):
        r0 = pl.multiple_of(rc * CONV_ROWS, CONV_ROWS)
        for lc in range(n_lane_chunks):
            ls = slice(lc * LANES, (lc + 1) * LANES)
            accs = [jnp.zeros((SUBLANES, LANES), F32) for _ in range(groups)]
            for k in range(CONV_K):
                whole, part = divmod(k + HALO - CONV_K // 2, SUBLANES)
                w = wb_ref[k * SUBLANES:(k + 1) * SUBLANES, ls]
                for j in range(groups):
                    row = r0 + (whole + j) * SUBLANES
                    if part == 0:
                        x = buf[pl.ds(row, SUBLANES), ls]
                    else:
                        x = shifted[part - 1, pl.ds(row, SUBLANES), ls]
                    accs[j] = accs[j] + w * x
            bias = cb_ref[:, ls]
            for j in range(groups):
                conv_scr[pl.ds(r0 + j * SUBLANES, SUBLANES), ls] = accs[j] + bias
        return carry

    lax.fori_loop(0, ts // CONV_ROWS, body, 0)

    v = conv_scr[...]
    mu = jnp.mean(v, axis=-1, keepdims=True)
    dv = v - mu
    var = jnp.mean(dv * dv, axis=-1, keepdims=True)
    y = dv * lax.rsqrt(var + EPS) * lg_ref[...] + lb_ref[...]
    o_ref[...] = _silu(y).astype(o_ref.dtype)


def _conv(p, wb, cb, lg, lb, *, n_batch, a_off, g_off):
    m = p.shape[0]
    cw = cb.shape[0]
    n = m // n_batch
    ts = min(256, n)
    nt = n // ts
    halo_per_tile = ts // HALO
    last_halo = m // HALO - 1

    def main(off):
        return pl.BlockSpec((ts, cw), lambda b, t: (b * nt + t, _blk(off, cw)))

    def prev(off):
        return pl.BlockSpec(
            (HALO, cw), lambda b, t: (jnp.maximum((b * nt + t) * halo_per_tile - 1, 0), _blk(off, cw)))

    def nxt(off):
        return pl.BlockSpec(
            (HALO, cw),
            lambda b, t: (jnp.minimum((b * nt + t + 1) * halo_per_tile, last_halo), _blk(off, cw)))

    vec = pl.BlockSpec((1, cw), lambda b, t: (0, 0))
    return pl.pallas_call(
        functools.partial(_conv_kernel, ts=ts),
        grid=(n_batch, nt),
        in_specs=[
            main(a_off), main(g_off), prev(a_off), prev(g_off), nxt(a_off), nxt(g_off),
            pl.BlockSpec((CONV_K * SUBLANES, cw), lambda b, t: (0, 0)),
            vec, vec, vec,
        ],
        out_specs=pl.BlockSpec((ts, cw), lambda b, t: (b * nt + t, 0)),
        out_shape=jax.ShapeDtypeStruct((m, cw), BF16),
        scratch_shapes=[
            pltpu.VMEM((ts + 2 * HALO, cw), F32),
            pltpu.VMEM((SUBLANES - 1, ts + 3 * SUBLANES, cw), F32),
            pltpu.VMEM((ts, cw), F32),
        ],
        compiler_params=_cparams(2),
        name="conformer_conv",
    )(p, p, p, p, p, p, wb, cb.reshape(1, cw), lg.reshape(1, cw), lb.reshape(1, cw))


def _merge_kernel(yd_ref, ym_ref, cv_ref, oa_ref, gf_ref, gc_ref, ga_ref, bf_ref, bc_ref, ba_ref,
                  wf_ref, wc_ref, wa_ref, o_ref, *, tiles_per_batch):
    def gate(g_ref, b_ref):
        return jax.nn.sigmoid(g_ref[...].astype(F32) + b_ref[...])

    if tiles_per_batch:
        first = (pl.program_id(0) % tiles_per_batch) < tiles_per_batch // 2
        yf = jnp.where(first, yd_ref[...], ym_ref[...])
    else:
        yf = yd_ref[...]
    acc = gate(gf_ref, bf_ref) * _dot(yf, wf_ref[...])
    acc = acc + gate(gc_ref, bc_ref) * _dot(cv_ref[...], wc_ref[...])
    acc = acc + gate(ga_ref, ba_ref) * _dot(oa_ref[...], wa_ref[...])
    o_ref[...] = acc.astype(o_ref.dtype)


def _merge(yf, cv, oa, p, b_gate, wf, wc, wa, layer, *, gate_off, rows_per_batch=None):
    m = cv.shape[0]
    d = wf.shape[2]
    tn = 512
    nj = d // tn
    g0 = _blk(gate_off, tn)
    if isinstance(yf, tuple):
        yd, ym = yf
        tm = min(1024, rows_per_batch // 2)
        tiles_per_batch = rows_per_batch // tm
        hpb = tiles_per_batch // 2

        def half_tile(i):
            return (i // tiles_per_batch) * hpb, i % tiles_per_batch

        def yd_map(i, j):
            base, it = half_tile(i)
            return base + jnp.minimum(it, hpb - 1), 0

        def ym_map(i, j):
            base, it = half_tile(i)
            return base + jnp.maximum(it - hpb, 0), 0

        y_specs = [pl.BlockSpec((tm, yd.shape[1]), yd_map), pl.BlockSpec((tm, ym.shape[1]), ym_map)]
    else:
        yd = ym = yf
        tm = min(1024, m)
        tiles_per_batch = 0
        y_specs = [pl.BlockSpec((tm, yf.shape[1]), lambda i, j: (i, 0))] * 2

    def act(a):
        return pl.BlockSpec((tm, a.shape[1]), lambda i, j: (i, 0))

    def gate(br):
        return pl.BlockSpec((tm, tn), lambda i, j: (i, g0 + br * nj + j))

    def bias(br):
        return pl.BlockSpec((None, 1, tn), lambda i, j: (layer, 0, br * nj + j))

    def weight(w):
        return _wspec(w, layer, tn)

    return pl.pallas_call(
        functools.partial(_merge_kernel, tiles_per_batch=tiles_per_batch),
        grid=(m // tm, nj),
        in_specs=y_specs + [act(cv), act(oa), gate(0), gate(1), gate(2),
                            bias(0), bias(1), bias(2), weight(wf), weight(wc), weight(wa)],
        out_specs=pl.BlockSpec((tm, tn), lambda i, j: (i, j)),
        out_shape=jax.ShapeDtypeStruct((m, d), BF16),
        compiler_params=_cparams(2),
        name="branch_merge",
    )(yd, ym, cv, oa, p, p, p, b_gate, b_gate, b_gate, wf, wc, wa)


def _final_norm_kernel(x_ref, g_ref, o_ref):
    o_ref[...] = _rms(x_ref[...], g_ref[...])


def _final_norm(x, g):
    m, d = x.shape
    tm = min(512, m)
    return pl.pallas_call(
        _final_norm_kernel,
        grid=(m // tm,),
        in_specs=[pl.BlockSpec((tm, d), lambda i: (i, 0)), pl.BlockSpec((1, d), lambda i: (0, 0))],
        out_specs=pl.BlockSpec((tm, d), lambda i: (i, 0)),
        out_shape=jax.ShapeDtypeStruct((m, d), F32),
        compiler_params=_cparams(1),
        name="final_rmsnorm",
    )(x, g.reshape(1, d))


def _rope_tables(n_tok):
    rows = n_tok // GRID_W
    row = np.repeat(np.arange(rows, dtype=np.float32), GRID_W)
    col = np.tile(np.arange(GRID_W, dtype=np.float32), rows)
    n_freq = QK_ROPE // 4
    inv_freq = (np.float32(ROPE_BASE) ** (-np.arange(n_freq, dtype=np.float32) / n_freq)).astype(np.float32)
    ang = np.concatenate([row[:, None] * inv_freq, col[:, None] * inv_freq], axis=-1)
    pad = np.zeros((n_tok, LANES - QK_ROPE), np.float32)
    cos = np.concatenate([np.cos(ang), np.cos(ang), pad], axis=-1).astype(np.float32)
    sin = np.concatenate([np.sin(ang), np.sin(ang), pad], axis=-1).astype(np.float32)
    return cos, sin


def _dft_cos_sin(n, k=None):
    k = np.arange(n) if k is None else np.asarray(k)
    ang = ((k[:, None] * np.arange(n)[None, :]) % n) * (2.0 * np.pi / n)
    scale = n ** -0.5
    return np.cos(ang) * scale, np.sin(ang) * scale


DFT_HALF_TILE = 512


def _half_dft_tables(n):
    th = min(DFT_HALF_TILE, n // 2)
    nh = n // 2 // th
    ext = th + BF16_ROWS
    k = np.arange(nh)[:, None] * th + np.arange(ext)[None, :]
    c, s = _dft_cos_sin(n, k.reshape(-1))
    sel = np.arange(ext)[None, :] == th - np.arange(th)[:, None]
    return tuple(jnp.asarray(a.astype(BF16))
                 for a in (c.reshape(nh, ext, n), s.reshape(nh, ext, n), sel))


def kernel(x, c, ctx, c_ctx, ada_w, ada_b, norm_mix_g, w_in, b_gate, w_fourier, conv_w, conv_b,
           conv_ln_g, conv_ln_b, w_conv_out, q_norm_g, w_uq, kv_norm_g, w_ukv, w_mla_o, w_out,
           norm_ffn_g, w_ffn_gate, w_ffn_up, w_ffn_down, final_norm_g):
    n_batch, seq, d = x.shape
    n_ctx = ctx.shape[1]
    depth = ada_w.shape[0]
    assert n_batch < SUBLANES
    q_lora = q_norm_g.shape[1]
    kv_lora = kv_norm_g.shape[1]
    f_width = w_fourier.shape[1]
    c_width = conv_w.shape[2]
    assert (f_width, c_width, q_lora, kv_lora) == (PX_A - PX_F, PX_G - PX_A, PX_GATE - PX_QD,
                                                    PX_KPE - PX_KVD)
    f_end = f_width
    c_end = f_end + 2 * c_width
    q_end = c_end + q_lora
    kv_end = q_end + kv_lora
    kpe_end = kv_end + QK_ROPE

    zeros = lambda k: jnp.zeros((depth, d, k), BF16)
    w_in_b = w_in.astype(BF16)
    w_in_r = jnp.concatenate([
        w_in_b[:, :, :c_end],
        w_in_b[:, :, q_end:kv_end],
        w_in_b[:, :, kv_end:kpe_end], zeros(LANES - QK_ROPE),
        zeros(PX_QD - PX_KPE - LANES),
        w_in_b[:, :, c_end:q_end],
        w_in_b[:, :, kpe_end:],
    ], axis=-1)
    assert w_in_r.shape[-1] == PX_COLS
    w_uq_h = w_uq.reshape(depth, q_lora, N_HEADS, QK_NOPE + QK_ROPE)
    w_uq_r = jnp.concatenate(
        [w_uq_h, jnp.zeros((depth, q_lora, N_HEADS, HEAD_W - QK_NOPE - QK_ROPE), w_uq.dtype)],
        axis=-1).reshape(depth, q_lora, N_HEADS * HEAD_W).astype(BF16)
    w_ukv_r = w_ukv.reshape(depth, kv_lora, N_HEADS, 2, QK_NOPE).transpose(0, 1, 3, 2, 4).reshape(
        depth, kv_lora, 2 * N_HEADS * QK_NOPE).astype(BF16)
    w_fourier_b = w_fourier.astype(BF16)
    w_conv_out_b = w_conv_out.astype(BF16)
    w_mla_o_b = w_mla_o.astype(BF16)
    w_out_b = w_out.astype(BF16)
    w_gate_b = w_ffn_gate.astype(BF16)
    w_up_b = w_ffn_up.astype(BF16)
    w_down_b = w_ffn_down.astype(BF16)
    conv_wb = jnp.repeat(conv_w, SUBLANES, axis=1)
    b_gate3 = b_gate.reshape(depth, 1, -1)

    cos_l, sin_l = (jnp.asarray(t) for t in _rope_tables(seq))
    cos_c = jnp.ones((n_batch * n_ctx, LANES), F32)
    sin_c = jnp.zeros((n_batch * n_ctx, LANES), F32)
    c128, s128 = _dft_cos_sin(F_GROUP_DIM)
    cs = jnp.asarray(np.concatenate([c128, s128], axis=1).astype(BF16))
    half_dft_l = _half_dft_tables(seq)
    cc, sc = _dft_cos_sin(n_ctx)
    dft_c = jnp.asarray(np.concatenate([cc, -sc], axis=1).astype(BF16))

    c8 = jnp.concatenate(
        [c, c_ctx[None, :], jnp.zeros((SUBLANES - n_batch - 1, d), c.dtype)], axis=0)
    mod_all = _ada(c8, ada_w, ada_b).reshape(depth, SUBLANES, 1, 6 * d)

    lat = _Rows(n_batch * seq, seq, None)
    cxt = _Rows(n_batch * n_ctx, None, n_batch)
    xs = x.reshape(n_batch * seq, d)
    cs_rows = ctx.reshape(n_batch * n_ctx, d)

    def mix_and_ffn(i, rows, h_in, p, o_attn, nb):
        mod3 = mod_all[i]
        if rows is lat:
            yf = _fourier_halves(p, cs, half_dft_l, n_batch=nb, f_off=PX_F)
        else:
            yf = _fourier(p, cs, dft_c, n_batch=nb, f_off=PX_F)
        cv = _conv(p, conv_wb[i], conv_b[i], conv_ln_g[i], conv_ln_b[i],
                   n_batch=nb, a_off=PX_A, g_off=PX_G)
        merged = _merge(yf, cv, o_attn, p, b_gate3, w_fourier_b, w_conv_out_b, w_mla_o_b, i,
                        gate_off=PX_GATE, rows_per_batch=rows.rows_per_batch)
        mixed = _resmm(merged, w_out_b, i, h_in, mod3, rows, gate_chunk=2, tm=2048, tn=512)
        hid = _ffn_up(mixed, norm_ffn_g[i], mod3, w_gate_b, w_up_b, i, rows)
        return _resmm(hid, w_down_b, i, mixed, mod3, rows, gate_chunk=5, tm=1024, tn=512)

    for i in range(depth):
        last = i == depth - 1
        mod3 = mod_all[i]
        px = _inproj(xs, norm_mix_g[i], mod3, w_in_r, i, lat, shift_chunk=0, tn=1536)
        if last:
            pc = _inproj(cs_rows, norm_mix_g[i], mod3, w_in_r, i, cxt, shift_chunk=0,
                         tn=PKV_COLS, col_off=PX_KVD, n=PKV_COLS)
            kvd_off, kpe_off = PKV_KVD, PKV_KPE
        else:
            pc = _inproj(cs_rows, norm_mix_g[i], mod3, w_in_r, i, cxt, shift_chunk=0, tn=1536)
            kvd_off, kpe_off = PX_KVD, PX_KPE

        q_x = _qup(px, q_norm_g[i], cos_l, sin_l, w_uq_r, i, qd_off=PX_QD)
        kv_x, kpe_x = _kvup(px, kv_norm_g[i], cos_l, sin_l, w_ukv_r, i,
                            kvd_off=PX_KVD, kpe_off=PX_KPE)
        kv_c, kpe_c = _kvup(pc, kv_norm_g[i], cos_c, sin_c, w_ukv_r, i,
                            kvd_off=kvd_off, kpe_off=kpe_off)
        o_x = _attention(q_x, kv_c, kpe_c, n_batch=n_batch, n_ctx=n_ctx,
                         q_rows_per_batch=seq, lat=(kv_x, kpe_x))
        if not last:
            q_c = _qup(pc, q_norm_g[i], cos_c, sin_c, w_uq_r, i, qd_off=PX_QD)
            o_c = _attention(q_c, kv_c, kpe_c, n_batch=n_batch, n_ctx=n_ctx,
                             q_rows_per_batch=n_ctx)
            cs_rows = mix_and_ffn(i, cxt, cs_rows, pc, o_c, n_batch)
        xs = mix_and_ffn(i, lat, xs, px, o_x, n_batch)

    return _final_norm(xs, final_norm_g).reshape(n_batch, seq, d)
```

```python
import functools

import numpy as np
import jax
import jax.numpy as jnp
from jax import lax
from jax.experimental import pallas as pl
from jax.experimental.pallas import tpu as pltpu

F32 = jnp.float32
BF16 = jnp.bfloat16

GRID_W = 64
N_HEADS = 16
QK_NOPE = 128
QK_ROPE = 64
V_DIM = 128
F_GROUPS = 8
F_GROUP_DIM = 128
CONV_K = 31
ROPE_BASE = 10000.0
ATTN_SCALE = (QK_NOPE + QK_ROPE) ** -0.5
Q_SCALE = ATTN_SCALE * 1.4426950408889634
EPS = 1e-6

LANES = 128
SUBLANES = 8
BF16_ROWS = 16
VMEM_LIMIT = 56 * 1024 * 1024

HEAD_W = 2 * LANES
HALO = BF16_ROWS

PX_F = 0
PX_A = 1024
PX_G = 2048
PX_KVD = 3072
PX_KPE = 3584
PX_QD = 3840
PX_GATE = 4608
PX_COLS = 10752
PKV_KVD = 0
PKV_KPE = 512
PKV_COLS = 768


def _cparams(n_axes):
    return pltpu.CompilerParams(
        dimension_semantics=("arbitrary",) * n_axes, vmem_limit_bytes=VMEM_LIMIT)


def _blk(off, width):
    assert off % width == 0, (off, width)
    return off // width


def _dot(a, b):
    return jnp.dot(a, b, preferred_element_type=F32)


def _silu(v):
    return v * jax.nn.sigmoid(v)


def _rms(x, g):
    r = lax.rsqrt(jnp.mean(x * x, axis=-1, keepdims=True) + EPS)
    return (x * r) * g


ROW_CHUNK_UNROLL = 8


def _by_row_chunks(n_rows, fn):
    def body(c, carry):
        fn(pl.ds(pl.multiple_of(c * BF16_ROWS, BF16_ROWS), BF16_ROWS))
        return carry

    lax.fori_loop(0, n_rows // BF16_ROWS, body, 0, unroll=ROW_CHUNK_UNROLL)


def _rope128(pe, cos, sin):
    lane = lax.broadcasted_iota(jnp.int32, pe.shape, 1)
    half = QK_ROPE // 2
    rot = jnp.where(lane < half, -pltpu.roll(pe, LANES - half, 1), pltpu.roll(pe, half, 1))
    return pe * cos + rot * sin


def _ada_kernel(c_ref, w_ref, b_ref, o_ref):
    a = _silu(c_ref[...]).astype(BF16)
    o_ref[...] = _dot(a, w_ref[...].astype(BF16)) + b_ref[...]


def _ada(c8, ada_w, ada_b):
    depth, d, n6 = ada_w.shape
    tn = 1024
    return pl.pallas_call(
        _ada_kernel,
        grid=(depth, n6 // tn),
        in_specs=[
            pl.BlockSpec((SUBLANES, d), lambda l, j: (0, 0)),
            pl.BlockSpec((None, d, tn), lambda l, j: (l, 0, j)),
            pl.BlockSpec((None, 1, tn), lambda l, j: (l, 0, j)),
        ],
        out_specs=pl.BlockSpec((None, SUBLANES, tn), lambda l, j: (l, 0, j)),
        out_shape=jax.ShapeDtypeStruct((depth, SUBLANES, n6), F32),
        compiler_params=_cparams(2),
        name="ada_mod",
    )(c8, ada_w, ada_b.reshape(depth, 1, n6))


RELAYOUT_ROWS = 256


def _relayout_kernel(src_ref, dst_ref, *, moves, zero_span):
    z0, zw = zero_span
    if zw:
        dst_ref[:, z0:z0 + zw] = jnp.zeros((dst_ref.shape[0], zw), dst_ref.dtype)
    for dst, src, width in moves:
        dst_ref[:, dst:dst + width] = src_ref[:, src:src + width].astype(dst_ref.dtype)


def _relayout_cols(w, out_cols, *, moves, zero_span, name):
    depth, k, cols = w.shape
    tr = RELAYOUT_ROWS
    return pl.pallas_call(
        functools.partial(_relayout_kernel, moves=moves, zero_span=zero_span),
        grid=(depth, k // tr),
        in_specs=[pl.BlockSpec((None, tr, cols), lambda l, r: (l, r, 0))],
        out_specs=pl.BlockSpec((None, tr, out_cols), lambda l, r: (l, r, 0)),
        out_shape=jax.ShapeDtypeStruct((depth, k, out_cols), BF16),
        compiler_params=_cparams(2),
        name=name,
    )(w)


class _Rows:
    def __init__(self, m, rows_per_batch, fixed_mod_row):
        self.m = m
        self.rows_per_batch = rows_per_batch
        self.fixed_mod_row = fixed_mod_row

    def tile(self, preferred):
        return min(preferred, self.rows_per_batch or self.m)

    def mod_row(self, tm):
        if self.fixed_mod_row is not None:
            row = self.fixed_mod_row
            return lambda i: row
        rpb = self.rows_per_batch
        assert rpb % tm == 0
        return lambda i: (i * tm) // rpb


def _mod_spec(rows, tm, d, chunk):
    row = rows.mod_row(tm)
    return pl.BlockSpec((None, 1, d), lambda i, j: (row(i), 0, chunk))


def _norm_modulate(x_ref, g_ref, sc_ref, sh_ref, h_ref):
    def chunk(rows):
        h = _rms(x_ref[rows, :], g_ref[...]) * (1.0 + sc_ref[...]) + sh_ref[...]
        h_ref[rows, :] = h.astype(h_ref.dtype)

    _by_row_chunks(x_ref.shape[0], chunk)


def _inproj_kernel(x_ref, g_ref, sc_ref, sh_ref, w_ref, o_ref, h_ref):
    @pl.when(pl.program_id(1) == 0)
    def _():
        _norm_modulate(x_ref, g_ref, sc_ref, sh_ref, h_ref)

    o_ref[...] = _dot(h_ref[...], w_ref[...]).astype(o_ref.dtype)


def _wspec(w, layer, tn, col0=0):
    return pl.BlockSpec((None, w.shape[1], tn), lambda i, j: (layer, 0, col0 + j))


def _inproj(x, g, mod3, w, layer, rows, *, shift_chunk, tn, col_off=0, n=None):
    m, d = x.shape
    n = n or w.shape[2]
    tm = rows.tile(1024)
    return pl.pallas_call(
        _inproj_kernel,
        grid=(m // tm, n // tn),
        in_specs=[
            pl.BlockSpec((tm, d), lambda i, j: (i, 0)),
            pl.BlockSpec((1, d), lambda i, j: (0, 0)),
            _mod_spec(rows, tm, d, shift_chunk + 1),
            _mod_spec(rows, tm, d, shift_chunk),
            _wspec(w, layer, tn, _blk(col_off, tn)),
        ],
        out_specs=pl.BlockSpec((tm, tn), lambda i, j: (i, j)),
        out_shape=jax.ShapeDtypeStruct((m, n), BF16),
        scratch_shapes=[pltpu.VMEM((tm, d), BF16)],
        compiler_params=_cparams(2),
        name="norm_mod_proj",
    )(x, g.reshape(1, d), mod3, mod3, w)


def _ffn_up_kernel(x_ref, g_ref, sc_ref, sh_ref, wg_ref, wu_ref, o_ref, h_ref):
    @pl.when(pl.program_id(1) == 0)
    def _():
        _norm_modulate(x_ref, g_ref, sc_ref, sh_ref, h_ref)

    hb = h_ref[...]
    o_ref[...] = (_silu(_dot(hb, wg_ref[...])) * _dot(hb, wu_ref[...])).astype(o_ref.dtype)


def _ffn_up(x, g, mod3, wg, wu, layer, rows):
    m, d = x.shape
    n = wg.shape[2]
    tm = rows.tile(1024)
    tn = 512
    return pl.pallas_call(
        _ffn_up_kernel,
        grid=(m // tm, n // tn),
        in_specs=[
            pl.BlockSpec((tm, d), lambda i, j: (i, 0)),
            pl.BlockSpec((1, d), lambda i, j: (0, 0)),
            _mod_spec(rows, tm, d, 4),
            _mod_spec(rows, tm, d, 3),
            _wspec(wg, layer, tn),
            _wspec(wu, layer, tn),
        ],
        out_specs=pl.BlockSpec((tm, tn), lambda i, j: (i, j)),
        out_shape=jax.ShapeDtypeStruct((m, n), BF16),
        scratch_shapes=[pltpu.VMEM((tm, d), BF16)],
        compiler_params=_cparams(2),
        name="ffn_gate_up",
    )(x, g.reshape(1, d), mod3, mod3, wg, wu)


def _resmm_kernel(a_ref, w_ref, res_ref, gate_ref, o_ref):
    o_ref[...] = res_ref[...] + gate_ref[...] * _dot(a_ref[...], w_ref[...])


def _resmm(a, w, layer, res, mod3, rows, *, gate_chunk, tm, tn):
    m, k = a.shape
    d = w.shape[2]
    tm = rows.tile(tm)
    row = rows.mod_row(tm)
    per_chunk = d // tn
    return pl.pallas_call(
        _resmm_kernel,
        grid=(m // tm, d // tn),
        in_specs=[
            pl.BlockSpec((tm, k), lambda i, j: (i, 0)),
            _wspec(w, layer, tn),
            pl.BlockSpec((tm, tn), lambda i, j: (i, j)),
            pl.BlockSpec((None, 1, tn), lambda i, j: (row(i), 0, gate_chunk * per_chunk + j)),
        ],
        out_specs=pl.BlockSpec((tm, tn), lambda i, j: (i, j)),
        out_shape=jax.ShapeDtypeStruct((m, d), F32),
        compiler_params=_cparams(2),
        name="proj_gated_residual",
    )(a, w, res, mod3)


def _qup_kernel(qd_ref, g_ref, cos_ref, sin_ref, w_ref, o_ref, h_ref, *, heads_per_tile):
    @pl.when(pl.program_id(1) == 0)
    def _():
        h_ref[...] = _rms(qd_ref[...].astype(F32), g_ref[...]).astype(BF16)

    acc = _dot(h_ref[...], w_ref[...]) * Q_SCALE
    cos = cos_ref[...]
    sin = sin_ref[...]
    for hh in range(heads_per_tile):
        c0 = hh * HEAD_W
        o_ref[:, c0:c0 + LANES] = acc[:, c0:c0 + LANES].astype(o_ref.dtype)
        pe = acc[:, c0 + LANES:c0 + HEAD_W]
        o_ref[:, c0 + LANES:c0 + HEAD_W] = _rope128(pe, cos, sin).astype(o_ref.dtype)


def _rope_spec(table, tm):
    tiles = table.shape[0] // tm
    return pl.BlockSpec((tm, LANES), lambda i, j: (i % tiles, 0))


def _qup(p, g, cos, sin, w, layer, *, qd_off):
    m = p.shape[0]
    _, kq, n = w.shape
    tm = min(1024, m, cos.shape[0])
    tn = 2048
    return pl.pallas_call(
        functools.partial(_qup_kernel, heads_per_tile=tn // HEAD_W),
        grid=(m // tm, n // tn),
        in_specs=[
            pl.BlockSpec((tm, kq), lambda i, j: (i, _blk(qd_off, kq))),
            pl.BlockSpec((1, kq), lambda i, j: (0, 0)),
            _rope_spec(cos, tm),
            _rope_spec(sin, tm),
            _wspec(w, layer, tn),
        ],
        out_specs=pl.BlockSpec((tm, tn), lambda i, j: (i, j)),
        out_shape=jax.ShapeDtypeStruct((m, n), BF16),
        scratch_shapes=[pltpu.VMEM((tm, kq), BF16)],
        compiler_params=_cparams(2),
        name="mla_q_up",
    )(p, g.reshape(1, kq), cos, sin, w)


def _kvup_kernel(kvd_ref, g_ref, kpe_ref, cos_ref, sin_ref, w_ref, o_ref, okpe_ref, h_ref):
    @pl.when(pl.program_id(1) == 0)
    def _():
        h_ref[...] = _rms(kvd_ref[...].astype(F32), g_ref[...]).astype(BF16)
        pe = kpe_ref[...].astype(F32)
        okpe_ref[...] = _rope128(pe, cos_ref[...], sin_ref[...]).astype(okpe_ref.dtype)

    o_ref[...] = _dot(h_ref[...], w_ref[...]).astype(o_ref.dtype)


def _kvup(p, g, cos, sin, w, layer, *, kvd_off, kpe_off):
    m = p.shape[0]
    _, kk, n = w.shape
    tm = min(1024, m, cos.shape[0])
    tn = 2048
    return pl.pallas_call(
        _kvup_kernel,
        grid=(m // tm, n // tn),
        in_specs=[
            pl.BlockSpec((tm, kk), lambda i, j: (i, _blk(kvd_off, kk))),
            pl.BlockSpec((1, kk), lambda i, j: (0, 0)),
            pl.BlockSpec((tm, LANES), lambda i, j: (i, _blk(kpe_off, LANES))),
            _rope_spec(cos, tm),
            _rope_spec(sin, tm),
            _wspec(w, layer, tn),
        ],
        out_specs=[
            pl.BlockSpec((tm, tn), lambda i, j: (i, j)),
            pl.BlockSpec((tm, LANES), lambda i, j: (i, 0)),
        ],
        out_shape=[
            jax.ShapeDtypeStruct((m, n), BF16),
            jax.ShapeDtypeStruct((m, LANES), BF16),
        ],
        scratch_shapes=[pltpu.VMEM((tm, kk), BF16)],
        compiler_params=_cparams(2),
        name="mla_kv_up",
    )(p, g.reshape(1, kk), p, cos, sin, w)


ATTN_TK = HEAD_W
ATTN_SUB = 256
ATTN_TQ = 1024


def _attn_kernel(*refs, n_lat, n_ctx):
    if n_lat:
        q_ref, kl_ref, kpl_ref, vl_ref, kc_ref, kpc_ref, vc_ref, o_ref, kt_scr, s_scr = refs
        segments = ((kl_ref, kpl_ref, vl_ref, n_lat), (kc_ref, kpc_ref, vc_ref, n_ctx))
    else:
        q_ref, kc_ref, kpc_ref, vc_ref, o_ref, kt_scr, s_scr = refs
        segments = ((kc_ref, kpc_ref, vc_ref, n_ctx),)
    tk = ATTN_TK
    chunks = []
    base = 0
    for k_ref, kp_ref, v_ref, n in segments:
        assert n % tk == 0
        chunks += [(base + r, k_ref, kp_ref, v_ref, r) for r in range(0, n, tk)]
        base += n

    @pl.when(pl.program_id(2) == 0)
    def _():
        for off, k_ref, kp_ref, _, r in chunks:
            kt_scr[0:LANES, off:off + tk] = k_ref[r:r + tk, :].T
            kt_scr[LANES:HEAD_W, off:off + tk] = kp_ref[r:r + tk, :].T

    tq = q_ref.shape[0]
    sub = min(ATTN_SUB, tq)
    for r0 in range(0, tq, sub):
        rows = slice(r0, r0 + sub)
        q = q_ref[rows, :]
        m128 = None
        for off, *_ in chunks:
            s = _dot(q, kt_scr[:, off:off + tk])
            s_scr[rows, off:off + tk] = s
            mx = jnp.maximum(s[:, :LANES], s[:, LANES:])
            m128 = mx if m128 is None else jnp.maximum(m128, mx)
        mb = jnp.broadcast_to(jnp.max(m128, axis=-1, keepdims=True), (sub, LANES))

        l128 = jnp.zeros((sub, LANES), F32)
        acc = jnp.zeros((sub, V_DIM), F32)
        for off, _, _, v_ref, r in chunks:
            p0 = jnp.exp2(s_scr[rows, off:off + LANES] - mb)
            p1 = jnp.exp2(s_scr[rows, off + LANES:off + tk] - mb)
            l128 = l128 + (p0 + p1)
            p = jnp.concatenate([p0, p1], axis=1).astype(BF16)
            acc = acc + _dot(p, v_ref[r:r + tk, :])
        l = jnp.sum(l128, axis=-1, keepdims=True)
        o_ref[rows, :] = (acc / l).astype(o_ref.dtype)


def _attention(q, kv_c, kpe_c, *, n_batch, n_ctx, q_rows_per_batch, lat=None):
    m = q.shape[0]
    nq_rows = q_rows_per_batch
    tq = min(ATTN_TQ, nq_rows)
    nq = nq_rows // tq

    def q_map(b, h, i):
        return b * nq + i, h

    def k_map(b, h, i):
        return b, h

    def kpe_map(b, h, i):
        return b, 0

    def v_map(b, h, i):
        return b, N_HEADS + h

    n_lat = 0
    in_specs = [pl.BlockSpec((tq, HEAD_W), q_map)]
    args = [q]
    if lat is not None:
        kv_l, kpe_l = lat
        n_lat = kv_l.shape[0] // n_batch
        in_specs += [
            pl.BlockSpec((n_lat, LANES), k_map),
            pl.BlockSpec((n_lat, LANES), kpe_map),
            pl.BlockSpec((n_lat, LANES), v_map),
        ]
        args += [kv_l, kpe_l, kv_l]
    in_specs += [
        pl.BlockSpec((n_ctx, LANES), k_map),
        pl.BlockSpec((n_ctx, LANES), kpe_map),
        pl.BlockSpec((n_ctx, LANES), v_map),
    ]
    args += [kv_c, kpe_c, kv_c]
    n_keys = n_lat + n_ctx
    return pl.pallas_call(
        functools.partial(_attn_kernel, n_lat=n_lat, n_ctx=n_ctx),
        grid=(n_batch, N_HEADS, nq),
        in_specs=in_specs,
        out_specs=pl.BlockSpec((tq, V_DIM), q_map),
        out_shape=jax.ShapeDtypeStruct((m, N_HEADS * V_DIM), BF16),
        scratch_shapes=[
            pltpu.VMEM((HEAD_W, n_keys), BF16),
            pltpu.VMEM((tq, n_keys), F32),
        ],
        compiler_params=_cparams(3),
        name="mla_attention",
    )(*args)


def _chan_dft_kernel(u_ref, cs_ref, o_ref):
    cs = cs_ref[...]
    for grp in range(F_GROUPS):
        c0 = grp * F_GROUP_DIM
        r = _dot(u_ref[:, c0:c0 + F_GROUP_DIM], cs)
        o_ref[0, :, c0:c0 + F_GROUP_DIM] = r[:, :F_GROUP_DIM].astype(o_ref.dtype)
        o_ref[1, :, c0:c0 + F_GROUP_DIM] = r[:, F_GROUP_DIM:].astype(o_ref.dtype)


def _mm_kernel(a_ref, b_ref, o_ref):
    o_ref[...] = _dot(a_ref[...], b_ref[...]).astype(o_ref.dtype)


def _half_dft_kernel(c_ref, s_ref, sel_ref, a_ref, b_ref, od_ref, om_ref):
    p = _dot(c_ref[...], a_ref[...])
    q = _dot(s_ref[...], b_ref[...])
    th = od_ref.shape[0]
    od_ref[...] = (p - q)[:th].astype(od_ref.dtype)
    om_ref[...] = _dot(sel_ref[...], (p + q).astype(BF16)).astype(om_ref.dtype)


def _channel_dft(p, cs, *, n_batch, f_off):
    m = p.shape[0]
    n = m // n_batch
    fw = F_GROUPS * F_GROUP_DIM
    tm = min(512, n)
    nt = n // tm
    return pl.pallas_call(
        _chan_dft_kernel,
        grid=(n_batch, nt),
        in_specs=[
            pl.BlockSpec((tm, fw), lambda b, t: (b * nt + t, _blk(f_off, fw))),
            pl.BlockSpec((F_GROUP_DIM, 2 * F_GROUP_DIM), lambda b, t: (0, 0)),
        ],
        out_specs=pl.BlockSpec((None, 2, tm, fw), lambda b, t: (b, 0, t, 0)),
        out_shape=jax.ShapeDtypeStruct((n_batch, 2, n, fw), BF16),
        compiler_params=_cparams(2),
        name="fourier_channel_dft",
    )(p, cs)


def _fourier_halves(p, cs, tables, *, n_batch, f_off):
    c_ext, s_ext, sel = tables
    nh, ext, n = c_ext.shape
    th = sel.shape[0]
    fw = F_GROUPS * F_GROUP_DIM
    z = _channel_dft(p, cs, n_batch=n_batch, f_off=f_off)
    tn = 512
    half = jax.ShapeDtypeStruct((n_batch * n // 2, fw), BF16)
    yd, ym = pl.pallas_call(
        _half_dft_kernel,
        grid=(nh, n_batch, fw // tn),
        in_specs=[
            pl.BlockSpec((None, ext, n), lambda j, b, c: (j, 0, 0)),
            pl.BlockSpec((None, ext, n), lambda j, b, c: (j, 0, 0)),
            pl.BlockSpec((th, ext), lambda j, b, c: (0, 0)),
            pl.BlockSpec((None, None, n, tn), lambda j, b, c: (b, 0, 0, c)),
            pl.BlockSpec((None, None, n, tn), lambda j, b, c: (b, 1, 0, c)),
        ],
        out_specs=[
            pl.BlockSpec((th, tn), lambda j, b, c: (b * nh + j, c)),
            pl.BlockSpec((th, tn), lambda j, b, c: (b * nh + (nh - 1 - j), c)),
        ],
        out_shape=[half, half],
        compiler_params=_cparams(3),
        name="fourier_position_half_dft",
    )(c_ext, s_ext, sel, z, z)
    return yd, ym


def _fourier(p, cs, dft, *, n_batch, f_off):
    m = p.shape[0]
    n = m // n_batch
    fw = F_GROUPS * F_GROUP_DIM
    tm = min(512, n)
    nt = n // tm
    z = _channel_dft(p, cs, n_batch=n_batch, f_off=f_off).reshape(n_batch, 2 * n, fw)
    tn = 512
    return pl.pallas_call(
        _mm_kernel,
        grid=(nt, n_batch, fw // tn),
        in_specs=[
            pl.BlockSpec((tm, 2 * n), lambda i, b, j: (i, 0)),
            pl.BlockSpec((None, 2 * n, tn), lambda i, b, j: (b, 0, j)),
        ],
        out_specs=pl.BlockSpec((tm, tn), lambda i, b, j: (b * nt + i, j)),
        out_shape=jax.ShapeDtypeStruct((m, fw), BF16),
        compiler_params=_cparams(3),
        name="fourier_position_dft",
    )(dft, z)


CONV_ROWS = 64


def _conv_kernel(a_ref, g_ref, ap_ref, gp_ref, an_ref, gn_ref, wb_ref, cb_ref, lg_ref, lb_ref,
                 o_ref, buf, shifted, conv_scr, *, ts):
    t = pl.program_id(1)
    last = pl.num_programs(1) - 1

    def glu(a, g):
        return a.astype(F32) * jax.nn.sigmoid(g.astype(F32))

    zeros = jnp.zeros((HALO, buf.shape[1]), F32)
    buf[0:HALO, :] = jnp.where(t > 0, glu(ap_ref[...], gp_ref[...]), zeros)
    buf[HALO:HALO + ts, :] = glu(a_ref[...], g_ref[...])
    buf[HALO + ts:2 * HALO + ts, :] = jnp.where(t < last, glu(an_ref[...], gn_ref[...]), zeros)

    span = ts + 3 * SUBLANES
    for s in range(1, SUBLANES):
        shifted[s - 1, 0:span, :] = buf[s:s + span, :]

    n_lane_chunks = buf.shape[1] // LANES
    groups = CONV_ROWS // SUBLANES

    def body(rc, carry):
        r0 = pl.multiple_of(rc * CONV_ROWS, CONV_ROWS)
        for lc in range(n_lane_chunks):
            ls = slice(lc * LANES, (lc + 1) * LANES)
            accs = [jnp.zeros((SUBLANES, LANES), F32) for _ in range(groups)]
            for k in range(CONV_K):
                whole, part = divmod(k + HALO - CONV_K // 2, SUBLANES)
                w = wb_ref[k * SUBLANES:(k + 1) * SUBLANES, ls]
                for j in range(groups):
                    row = r0 + (whole + j) * SUBLANES
                    if part == 0:
                        x = buf[pl.ds(row, SUBLANES), ls]
                    else:
                        x = shifted[part - 1, pl.ds(row, SUBLANES), ls]
                    accs[j] = accs[j] + w * x
            bias = cb_ref[:, ls]
            for j in range(groups):
                conv_scr[pl.ds(r0 + j * SUBLANES, SUBLANES), ls] = accs[j] + bias
        return carry

    lax.fori_loop(0, ts // CONV_ROWS, body, 0)

    v = conv_scr[...]
    mu = jnp.mean(v, axis=-1, keepdims=True)
    dv = v - mu
    var = jnp.mean(dv * dv, axis=-1, keepdims=True)
    y = dv * lax.rsqrt(var + EPS) * lg_ref[...] + lb_ref[...]
    o_ref[...] = _silu(y).astype(o_ref.dtype)


def _conv(p, wb, cb, lg, lb, *, n_batch, a_off, g_off):
    m = p.shape[0]
    cw = cb.shape[0]
    n = m // n_batch
    ts = min(256, n)
    nt = n // ts
    halo_per_tile = ts // HALO
    last_halo = m // HALO - 1

    def main(off):
        return pl.BlockSpec((ts, cw), lambda b, t: (b * nt + t, _blk(off, cw)))

    def prev(off):
        return pl.BlockSpec(
            (HALO, cw), lambda b, t: (jnp.maximum((b * nt + t) * halo_per_tile - 1, 0), _blk(off, cw)))

    def nxt(off):
        return pl.BlockSpec(
            (HALO, cw),
            lambda b, t: (jnp.minimum((b * nt + t + 1) * halo_per_tile, last_halo), _blk(off, cw)))

    vec = pl.BlockSpec((1, cw), lambda b, t: (0, 0))
    return pl.pallas_call(
        functools.partial(_conv_kernel, ts=ts),
        grid=(n_batch, nt),
        in_specs=[
            main(a_off), main(g_off), prev(a_off), prev(g_off), nxt(a_off), nxt(g_off),
            pl.BlockSpec((CONV_K * SUBLANES, cw), lambda b, t: (0, 0)),
            vec, vec, vec,
        ],
        out_specs=pl.BlockSpec((ts, cw), lambda b, t: (b * nt + t, 0)),
        out_shape=jax.ShapeDtypeStruct((m, cw), BF16),
        scratch_shapes=[
            pltpu.VMEM((ts + 2 * HALO, cw), F32),
            pltpu.VMEM((SUBLANES - 1, ts + 3 * SUBLANES, cw), F32),
            pltpu.VMEM((ts, cw), F32),
        ],
        compiler_params=_cparams(2),
        name="conformer_conv",
    )(p, p, p, p, p, p, wb, cb.reshape(1, cw), lg.reshape(1, cw), lb.reshape(1, cw))


def _merge_kernel(yd_ref, ym_ref, cv_ref, oa_ref, gf_ref, gc_ref, ga_ref, bf_ref, bc_ref, ba_ref,
                  wf_ref, wc_ref, wa_ref, o_ref, *, tiles_per_batch):
    def gate(g_ref, b_ref):
        return jax.nn.sigmoid(g_ref[...].astype(F32) + b_ref[...])

    if tiles_per_batch:
        first = (pl.program_id(0) % tiles_per_batch) < tiles_per_batch // 2
        yf = jnp.where(first, yd_ref[...], ym_ref[...])
    else:
        yf = yd_ref[...]
    acc = gate(gf_ref, bf_ref) * _dot(yf, wf_ref[...])
    acc = acc + gate(gc_ref, bc_ref) * _dot(cv_ref[...], wc_ref[...])
    acc = acc + gate(ga_ref, ba_ref) * _dot(oa_ref[...], wa_ref[...])
    o_ref[...] = acc.astype(o_ref.dtype)


def _merge(yf, cv, oa, p, b_gate, wf, wc, wa, layer, *, gate_off, rows_per_batch=None):
    m = cv.shape[0]
    d = wf.shape[2]
    tn = 512
    nj = d // tn
    g0 = _blk(gate_off, tn)
    if isinstance(yf, tuple):
        yd, ym = yf
        tm = min(1024, rows_per_batch // 2)
        tiles_per_batch = rows_per_batch // tm
        hpb = tiles_per_batch // 2

        def half_tile(i):
            return (i // tiles_per_batch) * hpb, i % tiles_per_batch

        def yd_map(i, j):
            base, it = half_tile(i)
            return base + jnp.minimum(it, hpb - 1), 0

        def ym_map(i, j):
            base, it = half_tile(i)
            return base + jnp.maximum(it - hpb, 0), 0

        y_specs = [pl.BlockSpec((tm, yd.shape[1]), yd_map), pl.BlockSpec((tm, ym.shape[1]), ym_map)]
    else:
        yd = ym = yf
        tm = min(1024, m)
        tiles_per_batch = 0
        y_specs = [pl.BlockSpec((tm, yf.shape[1]), lambda i, j: (i, 0))] * 2

    def act(a):
        return pl.BlockSpec((tm, a.shape[1]), lambda i, j: (i, 0))

    def gate(br):
        return pl.BlockSpec((tm, tn), lambda i, j: (i, g0 + br * nj + j))

    def bias(br):
        return pl.BlockSpec((None, 1, tn), lambda i, j: (layer, 0, br * nj + j))

    def weight(w):
        return _wspec(w, layer, tn)

    return pl.pallas_call(
        functools.partial(_merge_kernel, tiles_per_batch=tiles_per_batch),
        grid=(m // tm, nj),
        in_specs=y_specs + [act(cv), act(oa), gate(0), gate(1), gate(2),
                            bias(0), bias(1), bias(2), weight(wf), weight(wc), weight(wa)],
        out_specs=pl.BlockSpec((tm, tn), lambda i, j: (i, j)),
        out_shape=jax.ShapeDtypeStruct((m, d), BF16),
        compiler_params=_cparams(2),
        name="branch_merge",
    )(yd, ym, cv, oa, p, p, p, b_gate, b_gate, b_gate, wf, wc, wa)


def _final_norm_kernel(x_ref, g_ref, o_ref):
    o_ref[...] = _rms(x_ref[...], g_ref[...])


def _final_norm(x, g):
    m, d = x.shape
    tm = min(512, m)
    return pl.pallas_call(
        _final_norm_kernel,
        grid=(m // tm,),
        in_specs=[pl.BlockSpec((tm, d), lambda i: (i, 0)), pl.BlockSpec((1, d), lambda i: (0, 0))],
        out_specs=pl.BlockSpec((tm, d), lambda i: (i, 0)),
        out_shape=jax.ShapeDtypeStruct((m, d), F32),
        compiler_params=_cparams(1),
        name="final_rmsnorm",
    )(x, g.reshape(1, d))


def _rope_tables(n_tok):
    rows = n_tok // GRID_W
    row = np.repeat(np.arange(rows, dtype=np.float32), GRID_W)
    col = np.tile(np.arange(GRID_W, dtype=np.float32), rows)
    n_freq = QK_ROPE // 4
    inv_freq = (np.float32(ROPE_BASE) ** (-np.arange(n_freq, dtype=np.float32) / n_freq)).astype(np.float32)
    ang = np.concatenate([row[:, None] * inv_freq, col[:, None] * inv_freq], axis=-1)
    pad = np.zeros((n_tok, LANES - QK_ROPE), np.float32)
    cos = np.concatenate([np.cos(ang), np.cos(ang), pad], axis=-1).astype(np.float32)
    sin = np.concatenate([np.sin(ang), np.sin(ang), pad], axis=-1).astype(np.float32)
    return cos, sin


def _dft_cos_sin(n, k=None):
    k = np.arange(n) if k is None else np.asarray(k)
    ang = ((k[:, None] * np.arange(n)[None, :]) % n) * (2.0 * np.pi / n)
    scale = n ** -0.5
    return np.cos(ang) * scale, np.sin(ang) * scale


DFT_HALF_TILE = 512


def _half_dft_tables(n):
    th = min(DFT_HALF_TILE, n // 2)
    nh = n // 2 // th
    ext = th + BF16_ROWS
    k = np.arange(nh)[:, None] * th + np.arange(ext)[None, :]
    c, s = _dft_cos_sin(n, k.reshape(-1))
    sel = np.arange(ext)[None, :] == th - np.arange(th)[:, None]
    return tuple(jnp.asarray(a.astype(BF16))
                 for a in (c.reshape(nh, ext, n), s.reshape(nh, ext, n), sel))


def kernel(x, c, ctx, c_ctx, ada_w, ada_b, norm_mix_g, w_in, b_gate, w_fourier, conv_w, conv_b,
           conv_ln_g, conv_ln_b, w_conv_out, q_norm_g, w_uq, kv_norm_g, w_ukv, w_mla_o, w_out,
           norm_ffn_g, w_ffn_gate, w_ffn_up, w_ffn_down, final_norm_g):
    n_batch, seq, d = x.shape
    n_ctx = ctx.shape[1]
    depth = ada_w.shape[0]
    assert n_batch < SUBLANES
    q_lora = q_norm_g.shape[1]
    kv_lora = kv_norm_g.shape[1]
    f_width = w_fourier.shape[1]
    c_width = conv_w.shape[2]
    assert (f_width, c_width, q_lora, kv_lora) == (PX_A - PX_F, PX_G - PX_A, PX_GATE - PX_QD,
                                                    PX_KPE - PX_KVD)
    f_end = f_width
    c_end = f_end + 2 * c_width
    q_end = c_end + q_lora
    kv_end = q_end + kv_lora
    kpe_end = kv_end + QK_ROPE

    w_in_r = _relayout_cols(w_in, PX_COLS, name="w_in_relayout", moves=(
        (PX_F, 0, c_end),
        (PX_KVD, q_end, kv_lora),
        (PX_KPE, kv_end, QK_ROPE),
        (PX_QD, c_end, q_lora),
        (PX_GATE, kpe_end, w_in.shape[2] - kpe_end),
    ), zero_span=(PX_KPE, PX_QD - PX_KPE))
    qk = QK_NOPE + QK_ROPE
    w_uq_r = _relayout_cols(
        w_uq, N_HEADS * HEAD_W, name="w_uq_relayout",
        moves=tuple((h * HEAD_W, h * qk, qk) for h in range(N_HEADS)),
        zero_span=(0, N_HEADS * HEAD_W))
    w_ukv_r = _relayout_cols(
        w_ukv, 2 * N_HEADS * QK_NOPE, name="w_ukv_relayout",
        moves=tuple(((s * N_HEADS + h) * QK_NOPE, (2 * h + s) * QK_NOPE, QK_NOPE)
                    for h in range(N_HEADS) for s in range(2)),
        zero_span=(0, 0))
    w_fourier_b = w_fourier.astype(BF16)
    w_conv_out_b = w_conv_out.astype(BF16)
    w_mla_o_b = w_mla_o.astype(BF16)
    w_out_b = w_out.astype(BF16)
    w_gate_b = w_ffn_gate.astype(BF16)
    w_up_b = w_ffn_up.astype(BF16)
    w_down_b = w_ffn_down.astype(BF16)
    conv_wb = jnp.repeat(conv_w, SUBLANES, axis=1)
    b_gate3 = b_gate.reshape(depth, 1, -1)

    cos_l, sin_l = (jnp.asarray(t) for t in _rope_tables(seq))
    cos_c = jnp.ones((n_batch * n_ctx, LANES), F32)
    sin_c = jnp.zeros((n_batch * n_ctx, LANES), F32)
    c128, s128 = _dft_cos_sin(F_GROUP_DIM)
    cs = jnp.asarray(np.concatenate([c128, s128], axis=1).astype(BF16))
    half_dft_l = _half_dft_tables(seq)
    cc, sc = _dft_cos_sin(n_ctx)
    dft_c = jnp.asarray(np.concatenate([cc, -sc], axis=1).astype(BF16))

    c8 = jnp.concatenate(
        [c, c_ctx[None, :], jnp.zeros((SUBLANES - n_batch - 1, d), c.dtype)], axis=0)
    mod_all = _ada(c8, ada_w, ada_b).reshape(depth, SUBLANES, 1, 6 * d)

    lat = _Rows(n_batch * seq, seq, None)
    cxt = _Rows(n_batch * n_ctx, None, n_batch)
    xs = x.reshape(n_batch * seq, d)
    cs_rows = ctx.reshape(n_batch * n_ctx, d)

    def mix_and_ffn(i, rows, h_in, p, o_attn, nb):
        mod3 = mod_all[i]
        if rows is lat:
            yf = _fourier_halves(p, cs, half_dft_l, n_batch=nb, f_off=PX_F)
        else:
            yf = _fourier(p, cs, dft_c, n_batch=nb, f_off=PX_F)
        cv = _conv(p, conv_wb[i], conv_b[i], conv_ln_g[i], conv_ln_b[i],
                   n_batch=nb, a_off=PX_A, g_off=PX_G)
        merged = _merge(yf, cv, o_attn, p, b_gate3, w_fourier_b, w_conv_out_b, w_mla_o_b, i,
                        gate_off=PX_GATE, rows_per_batch=rows.rows_per_batch)
        mixed = _resmm(merged, w_out_b, i, h_in, mod3, rows, gate_chunk=2, tm=2048, tn=512)
        hid = _ffn_up(mixed, norm_ffn_g[i], mod3, w_gate_b, w_up_b, i, rows)
        return _resmm(hid, w_down_b, i, mixed, mod3, rows, gate_chunk=5, tm=1024, tn=512)

    for i in range(depth):
        last = i == depth - 1
        mod3 = mod_all[i]
        px = _inproj(xs, norm_mix_g[i], mod3, w_in_r, i, lat, shift_chunk=0, tn=1536)
        if last:
            pc = _inproj(cs_rows, norm_mix_g[i], mod3, w_in_r, i, cxt, shift_chunk=0,
                         tn=PKV_COLS, col_off=PX_KVD, n=PKV_COLS)
            kvd_off, kpe_off = PKV_KVD, PKV_KPE
        else:
            pc = _inproj(cs_rows, norm_mix_g[i], mod3, w_in_r, i, cxt, shift_chunk=0, tn=1536)
            kvd_off, kpe_off = PX_KVD, PX_KPE

        q_x = _qup(px, q_norm_g[i], cos_l, sin_l, w_uq_r, i, qd_off=PX_QD)
        kv_x, kpe_x = _kvup(px, kv_norm_g[i], cos_l, sin_l, w_ukv_r, i,
                            kvd_off=PX_KVD, kpe_off=PX_KPE)
        kv_c, kpe_c = _kvup(pc, kv_norm_g[i], cos_c, sin_c, w_ukv_r, i,
                            kvd_off=kvd_off, kpe_off=kpe_off)
        o_x = _attention(q_x, kv_c, kpe_c, n_batch=n_batch, n_ctx=n_ctx,
                         q_rows_per_batch=seq, lat=(kv_x, kpe_x))
        if not last:
            q_c = _qup(pc, q_norm_g[i], cos_c, sin_c, w_uq_r, i, qd_off=PX_QD)
            o_c = _attention(q_c, kv_c, kpe_c, n_batch=n_batch, n_ctx=n_ctx,
                             q_rows_per_batch=n_ctx)
            cs_rows = mix_and_ffn(i, cxt, cs_rows, pc, o_c, n_batch)
        xs = mix_and_ffn(i, lat, xs, px, o_x, n_batch)

    return _final_norm(xs, final_norm_g).reshape(n_batch, seq, d)
```

```python
import functools

import numpy as np
import jax
import jax.numpy as jnp
from jax import lax
from jax.experimental import pallas as pl
from jax.experimental.pallas import tpu as pltpu

F32 = jnp.float32
BF16 = jnp.bfloat16

GRID_W = 64
N_HEADS = 16
QK_NOPE = 128
QK_ROPE = 64
V_DIM = 128
F_GROUPS = 8
F_GROUP_DIM = 128
CONV_K = 31
ROPE_BASE = 10000.0
ATTN_SCALE = (QK_NOPE + QK_ROPE) ** -0.5
Q_SCALE = ATTN_SCALE * 1.4426950408889634
EPS = 1e-6

LANES = 128
SUBLANES = 8
BF16_ROWS = 16
VMEM_LIMIT = 56 * 1024 * 1024

HEAD_W = 2 * LANES
HALO = BF16_ROWS

PX_F = 0
PX_A = 1024
PX_G = 2048
PX_KVD = 3072
PX_KPE = 3584
PX_QD = 3840
PX_GATE = 4608
PX_COLS = 10752
PKV_KVD = 0
PKV_KPE = 512
PKV_COLS = 768


def _cparams(n_axes):
    return pltpu.CompilerParams(
        dimension_semantics=("arbitrary",) * n_axes, vmem_limit_bytes=VMEM_LIMIT)


def _blk(off, width):
    assert off % width == 0, (off, width)
    return off // width


def _dot(a, b):
    return jnp.dot(a, b, preferred_element_type=F32)


def _silu(v):
    return v * jax.nn.sigmoid(v)


def _rms(x, g):
    r = lax.rsqrt(jnp.mean(x * x, axis=-1, keepdims=True) + EPS)
    return (x * r) * g


ROW_CHUNK_UNROLL = 8


def _by_row_chunks(n_rows, fn):
    def body(c, carry):
        fn(pl.ds(pl.multiple_of(c * BF16_ROWS, BF16_ROWS), BF16_ROWS))
        return carry

    lax.fori_loop(0, n_rows // BF16_ROWS, body, 0, unroll=ROW_CHUNK_UNROLL)


def _rope128(pe, cos, sin):
    lane = lax.broadcasted_iota(jnp.int32, pe.shape, 1)
    half = QK_ROPE // 2
    rot = jnp.where(lane < half, -pltpu.roll(pe, LANES - half, 1), pltpu.roll(pe, half, 1))
    return pe * cos + rot * sin


def _ada_kernel(c_ref, w_ref, b_ref, o_ref):
    a = _silu(c_ref[...]).astype(BF16)
    o_ref[...] = _dot(a, w_ref[...].astype(BF16)) + b_ref[...]


def _ada(c8, ada_w, ada_b):
    depth, d, n6 = ada_w.shape
    tn = 1024
    return pl.pallas_call(
        _ada_kernel,
        grid=(depth, n6 // tn),
        in_specs=[
            pl.BlockSpec((SUBLANES, d), lambda l, j: (0, 0)),
            pl.BlockSpec((None, d, tn), lambda l, j: (l, 0, j)),
            pl.BlockSpec((None, 1, tn), lambda l, j: (l, 0, j)),
        ],
        out_specs=pl.BlockSpec((None, SUBLANES, tn), lambda l, j: (l, 0, j)),
        out_shape=jax.ShapeDtypeStruct((depth, SUBLANES, n6), F32),
        compiler_params=_cparams(2),
        name="ada_mod",
    )(c8, ada_w, ada_b.reshape(depth, 1, n6))


RELAYOUT_ROWS = 256


def _relayout_kernel(src_ref, dst_ref, *, moves, zero_span):
    z0, zw = zero_span
    if zw:
        dst_ref[:, z0:z0 + zw] = jnp.zeros((dst_ref.shape[0], zw), dst_ref.dtype)
    for dst, src, width in moves:
        dst_ref[:, dst:dst + width] = src_ref[:, src:src + width].astype(dst_ref.dtype)


def _relayout_rows_kernel(src_ref, dst_ref, *, moves, zero_span):
    z0, zw = zero_span
    if zw:
        dst_ref[z0:z0 + zw, :] = jnp.zeros((zw, dst_ref.shape[1]), dst_ref.dtype)
    for dst, src, width in moves:
        dst_ref[dst:dst + width, :] = src_ref[src:src + width, :].astype(dst_ref.dtype)


def _relayout_rows(w, out_rows, *, moves, zero_span, name):
    depth, rows, k = w.shape
    tk = RELAYOUT_ROWS
    return pl.pallas_call(
        functools.partial(_relayout_rows_kernel, moves=moves, zero_span=zero_span),
        grid=(depth, k // tk),
        in_specs=[pl.BlockSpec((None, rows, tk), lambda l, c: (l, 0, c))],
        out_specs=pl.BlockSpec((None, out_rows, tk), lambda l, c: (l, 0, c)),
        out_shape=jax.ShapeDtypeStruct((depth, out_rows, k), BF16),
        compiler_params=_cparams(2),
        name=name,
    )(w)


def _relayout_cols(w, out_cols, *, moves, zero_span, name):
    depth, k, cols = w.shape
    tr = RELAYOUT_ROWS
    return pl.pallas_call(
        functools.partial(_relayout_kernel, moves=moves, zero_span=zero_span),
        grid=(depth, k // tr),
        in_specs=[pl.BlockSpec((None, tr, cols), lambda l, r: (l, r, 0))],
        out_specs=pl.BlockSpec((None, tr, out_cols), lambda l, r: (l, r, 0)),
        out_shape=jax.ShapeDtypeStruct((depth, k, out_cols), BF16),
        compiler_params=_cparams(2),
        name=name,
    )(w)


class _Rows:
    def __init__(self, m, rows_per_batch, fixed_mod_row):
        self.m = m
        self.rows_per_batch = rows_per_batch
        self.fixed_mod_row = fixed_mod_row

    def tile(self, preferred):
        return min(preferred, self.rows_per_batch or self.m)

    def mod_row(self, tm):
        if self.fixed_mod_row is not None:
            row = self.fixed_mod_row
            return lambda i: row
        rpb = self.rows_per_batch
        assert rpb % tm == 0
        return lambda i: (i * tm) // rpb


def _mod_spec(rows, tm, d, chunk):
    row = rows.mod_row(tm)
    return pl.BlockSpec((None, 1, d), lambda i, j: (row(i), 0, chunk))


def _norm_modulate(x_ref, g_ref, sc_ref, sh_ref, h_ref):
    def chunk(rows):
        h = _rms(x_ref[rows, :], g_ref[...]) * (1.0 + sc_ref[...]) + sh_ref[...]
        h_ref[rows, :] = h.astype(h_ref.dtype)

    _by_row_chunks(x_ref.shape[0], chunk)


def _inproj_kernel(x_ref, g_ref, sc_ref, sh_ref, w_ref, o_ref, h_ref):
    @pl.when(pl.program_id(1) == 0)
    def _():
        _norm_modulate(x_ref, g_ref, sc_ref, sh_ref, h_ref)

    o_ref[...] = lax.dot_general(h_ref[...], w_ref[...], (((1,), (1,)), ((), ())),
                                 preferred_element_type=F32).astype(o_ref.dtype)


def _wspec(w, layer, tn, col0=0):
    return pl.BlockSpec((None, w.shape[1], tn), lambda i, j: (layer, 0, col0 + j))


def _inproj(x, g, mod3, w, layer, rows, *, shift_chunk, tn, col_off=0, n=None):
    m, d = x.shape
    n = n or w.shape[1]
    tm = rows.tile(1024)
    return pl.pallas_call(
        _inproj_kernel,
        grid=(m // tm, n // tn),
        in_specs=[
            pl.BlockSpec((tm, d), lambda i, j: (i, 0)),
            pl.BlockSpec((1, d), lambda i, j: (0, 0)),
            _mod_spec(rows, tm, d, shift_chunk + 1),
            _mod_spec(rows, tm, d, shift_chunk),
            pl.BlockSpec((None, tn, d), lambda i, j: (layer, _blk(col_off, tn) + j, 0)),
        ],
        out_specs=pl.BlockSpec((tm, tn), lambda i, j: (i, j)),
        out_shape=jax.ShapeDtypeStruct((m, n), BF16),
        scratch_shapes=[pltpu.VMEM((tm, d), BF16)],
        compiler_params=_cparams(2),
        name="norm_mod_proj",
    )(x, g.reshape(1, d), mod3, mod3, w)


def _ffn_up_kernel(x_ref, g_ref, sc_ref, sh_ref, wg_ref, wu_ref, o_ref, h_ref):
    @pl.when(pl.program_id(1) == 0)
    def _():
        _norm_modulate(x_ref, g_ref, sc_ref, sh_ref, h_ref)

    hb = h_ref[...]
    o_ref[...] = (_silu(_dot(hb, wg_ref[...])) * _dot(hb, wu_ref[...])).astype(o_ref.dtype)


def _ffn_up(x, g, mod3, wg, wu, layer, rows):
    m, d = x.shape
    n = wg.shape[2]
    tm = rows.tile(1024)
    tn = 512
    return pl.pallas_call(
        _ffn_up_kernel,
        grid=(m // tm, n // tn),
        in_specs=[
            pl.BlockSpec((tm, d), lambda i, j: (i, 0)),
            pl.BlockSpec((1, d), lambda i, j: (0, 0)),
            _mod_spec(rows, tm, d, 4),
            _mod_spec(rows, tm, d, 3),
            _wspec(wg, layer, tn),
            _wspec(wu, layer, tn),
        ],
        out_specs=pl.BlockSpec((tm, tn), lambda i, j: (i, j)),
        out_shape=jax.ShapeDtypeStruct((m, n), BF16),
        scratch_shapes=[pltpu.VMEM((tm, d), BF16)],
        compiler_params=_cparams(2),
        name="ffn_gate_up",
    )(x, g.reshape(1, d), mod3, mod3, wg, wu)


def _resmm_kernel(a_ref, w_ref, res_ref, gate_ref, o_ref):
    o_ref[...] = res_ref[...] + gate_ref[...] * _dot(a_ref[...], w_ref[...])


def _resmm(a, w, layer, res, mod3, rows, *, gate_chunk, tm, tn):
    m, k = a.shape
    d = w.shape[2]
    tm = rows.tile(tm)
    row = rows.mod_row(tm)
    per_chunk = d // tn
    return pl.pallas_call(
        _resmm_kernel,
        grid=(m // tm, d // tn),
        in_specs=[
            pl.BlockSpec((tm, k), lambda i, j: (i, 0)),
            _wspec(w, layer, tn),
            pl.BlockSpec((tm, tn), lambda i, j: (i, j)),
            pl.BlockSpec((None, 1, tn), lambda i, j: (row(i), 0, gate_chunk * per_chunk + j)),
        ],
        out_specs=pl.BlockSpec((tm, tn), lambda i, j: (i, j)),
        out_shape=jax.ShapeDtypeStruct((m, d), F32),
        compiler_params=_cparams(2),
        name="proj_gated_residual",
    )(a, w, res, mod3)


def _qup_kernel(qd_ref, g_ref, cos_ref, sin_ref, w_ref, o_ref, h_ref, *, heads_per_tile):
    @pl.when(pl.program_id(1) == 0)
    def _():
        h_ref[...] = _rms(qd_ref[...].astype(F32), g_ref[...]).astype(BF16)

    acc = _dot(h_ref[...], w_ref[...]) * Q_SCALE
    cos = cos_ref[...]
    sin = sin_ref[...]
    for hh in range(heads_per_tile):
        c0 = hh * HEAD_W
        o_ref[:, c0:c0 + LANES] = acc[:, c0:c0 + LANES].astype(o_ref.dtype)
        pe = acc[:, c0 + LANES:c0 + HEAD_W]
        o_ref[:, c0 + LANES:c0 + HEAD_W] = _rope128(pe, cos, sin).astype(o_ref.dtype)


def _rope_spec(table, tm):
    tiles = table.shape[0] // tm
    return pl.BlockSpec((tm, LANES), lambda i, j: (i % tiles, 0))


def _qup(p, g, cos, sin, w, layer, *, qd_off):
    m = p.shape[0]
    _, kq, n = w.shape
    tm = min(1024, m, cos.shape[0])
    tn = 2048
    return pl.pallas_call(
        functools.partial(_qup_kernel, heads_per_tile=tn // HEAD_W),
        grid=(m // tm, n // tn),
        in_specs=[
            pl.BlockSpec((tm, kq), lambda i, j: (i, _blk(qd_off, kq))),
            pl.BlockSpec((1, kq), lambda i, j: (0, 0)),
            _rope_spec(cos, tm),
            _rope_spec(sin, tm),
            _wspec(w, layer, tn),
        ],
        out_specs=pl.BlockSpec((tm, tn), lambda i, j: (i, j)),
        out_shape=jax.ShapeDtypeStruct((m, n), BF16),
        scratch_shapes=[pltpu.VMEM((tm, kq), BF16)],
        compiler_params=_cparams(2),
        name="mla_q_up",
    )(p, g.reshape(1, kq), cos, sin, w)


def _kvup_kernel(kvd_ref, g_ref, kpe_ref, cos_ref, sin_ref, w_ref, o_ref, okpe_ref, h_ref):
    @pl.when(pl.program_id(1) == 0)
    def _():
        h_ref[...] = _rms(kvd_ref[...].astype(F32), g_ref[...]).astype(BF16)
        pe = kpe_ref[...].astype(F32)
        okpe_ref[...] = _rope128(pe, cos_ref[...], sin_ref[...]).astype(okpe_ref.dtype)

    o_ref[...] = _dot(h_ref[...], w_ref[...]).astype(o_ref.dtype)


def _kvup(p, g, cos, sin, w, layer, *, kvd_off, kpe_off):
    m = p.shape[0]
    _, kk, n = w.shape
    tm = min(1024, m, cos.shape[0])
    tn = 2048
    return pl.pallas_call(
        _kvup_kernel,
        grid=(m // tm, n // tn),
        in_specs=[
            pl.BlockSpec((tm, kk), lambda i, j: (i, _blk(kvd_off, kk))),
            pl.BlockSpec((1, kk), lambda i, j: (0, 0)),
            pl.BlockSpec((tm, LANES), lambda i, j: (i, _blk(kpe_off, LANES))),
            _rope_spec(cos, tm),
            _rope_spec(sin, tm),
            _wspec(w, layer, tn),
        ],
        out_specs=[
            pl.BlockSpec((tm, tn), lambda i, j: (i, j)),
            pl.BlockSpec((tm, LANES), lambda i, j: (i, 0)),
        ],
        out_shape=[
            jax.ShapeDtypeStruct((m, n), BF16),
            jax.ShapeDtypeStruct((m, LANES), BF16),
        ],
        scratch_shapes=[pltpu.VMEM((tm, kk), BF16)],
        compiler_params=_cparams(2),
        name="mla_kv_up",
    )(p, g.reshape(1, kk), p, cos, sin, w)


ATTN_TK = HEAD_W
ATTN_SUB = 256
ATTN_TQ = 1024


def _attn_kernel(*refs, n_lat, n_ctx):
    if n_lat:
        q_ref, kl_ref, kpl_ref, vl_ref, kc_ref, kpc_ref, vc_ref, o_ref, kt_scr, s_scr = refs
        segments = ((kl_ref, kpl_ref, vl_ref, n_lat), (kc_ref, kpc_ref, vc_ref, n_ctx))
    else:
        q_ref, kc_ref, kpc_ref, vc_ref, o_ref, kt_scr, s_scr = refs
        segments = ((kc_ref, kpc_ref, vc_ref, n_ctx),)
    tk = ATTN_TK
    chunks = []
    base = 0
    for k_ref, kp_ref, v_ref, n in segments:
        assert n % tk == 0
        chunks += [(base + r, k_ref, kp_ref, v_ref, r) for r in range(0, n, tk)]
        base += n

    @pl.when(pl.program_id(2) == 0)
    def _():
        for off, k_ref, kp_ref, _, r in chunks:
            kt_scr[0:LANES, off:off + tk] = k_ref[r:r + tk, :].T
            kt_scr[LANES:HEAD_W, off:off + tk] = kp_ref[r:r + tk, :].T

    tq = q_ref.shape[0]
    sub = min(ATTN_SUB, tq)
    for r0 in range(0, tq, sub):
        rows = slice(r0, r0 + sub)
        q = q_ref[rows, :]
        m128 = None
        for off, *_ in chunks:
            s = _dot(q, kt_scr[:, off:off + tk])
            s_scr[rows, off:off + tk] = s
            mx = jnp.maximum(s[:, :LANES], s[:, LANES:])
            m128 = mx if m128 is None else jnp.maximum(m128, mx)
        mb = jnp.broadcast_to(jnp.max(m128, axis=-1, keepdims=True), (sub, LANES))

        l128 = jnp.zeros((sub, LANES), F32)
        acc = jnp.zeros((sub, V_DIM), F32)
        for off, _, _, v_ref, r in chunks:
            p0 = jnp.exp2(s_scr[rows, off:off + LANES] - mb)
            p1 = jnp.exp2(s_scr[rows, off + LANES:off + tk] - mb)
            l128 = l128 + (p0 + p1)
            p = jnp.concatenate([p0, p1], axis=1).astype(BF16)
            acc = acc + _dot(p, v_ref[r:r + tk, :])
        l = jnp.sum(l128, axis=-1, keepdims=True)
        o_ref[rows, :] = (acc / l).astype(o_ref.dtype)


def _attention(q, kv_c, kpe_c, *, n_batch, n_ctx, q_rows_per_batch, lat=None):
    m = q.shape[0]
    nq_rows = q_rows_per_batch
    tq = min(ATTN_TQ, nq_rows)
    nq = nq_rows // tq

    def q_map(b, h, i):
        return b * nq + i, h

    def k_map(b, h, i):
        return b, h

    def kpe_map(b, h, i):
        return b, 0

    def v_map(b, h, i):
        return b, N_HEADS + h

    n_lat = 0
    in_specs = [pl.BlockSpec((tq, HEAD_W), q_map)]
    args = [q]
    if lat is not None:
        kv_l, kpe_l = lat
        n_lat = kv_l.shape[0] // n_batch
        in_specs += [
            pl.BlockSpec((n_lat, LANES), k_map),
            pl.BlockSpec((n_lat, LANES), kpe_map),
            pl.BlockSpec((n_lat, LANES), v_map),
        ]
        args += [kv_l, kpe_l, kv_l]
    in_specs += [
        pl.BlockSpec((n_ctx, LANES), k_map),
        pl.BlockSpec((n_ctx, LANES), kpe_map),
        pl.BlockSpec((n_ctx, LANES), v_map),
    ]
    args += [kv_c, kpe_c, kv_c]
    n_keys = n_lat + n_ctx
    return pl.pallas_call(
        functools.partial(_attn_kernel, n_lat=n_lat, n_ctx=n_ctx),
        grid=(n_batch, N_HEADS, nq),
        in_specs=in_specs,
        out_specs=pl.BlockSpec((tq, V_DIM), q_map),
        out_shape=jax.ShapeDtypeStruct((m, N_HEADS * V_DIM), BF16),
        scratch_shapes=[
            pltpu.VMEM((HEAD_W, n_keys), BF16),
            pltpu.VMEM((tq, n_keys), F32),
        ],
        compiler_params=_cparams(3),
        name="mla_attention",
    )(*args)


def _chan_dft_kernel(u_ref, cs_ref, o_ref):
    cs = cs_ref[...]
    for grp in range(F_GROUPS):
        c0 = grp * F_GROUP_DIM
        r = _dot(u_ref[:, c0:c0 + F_GROUP_DIM], cs)
        o_ref[0, :, c0:c0 + F_GROUP_DIM] = r[:, :F_GROUP_DIM].astype(o_ref.dtype)
        o_ref[1, :, c0:c0 + F_GROUP_DIM] = r[:, F_GROUP_DIM:].astype(o_ref.dtype)


def _mm_kernel(a_ref, b_ref, o_ref):
    o_ref[...] = _dot(a_ref[...], b_ref[...]).astype(o_ref.dtype)


def _half_dft_kernel(c_ref, s_ref, sel_ref, a_ref, b_ref, od_ref, om_ref):
    p = _dot(c_ref[...], a_ref[...])
    q = _dot(s_ref[...], b_ref[...])
    th = od_ref.shape[0]
    od_ref[...] = (p - q)[:th].astype(od_ref.dtype)
    om_ref[...] = _dot(sel_ref[...], (p + q).astype(BF16)).astype(om_ref.dtype)


def _channel_dft(p, cs, *, n_batch, f_off):
    m = p.shape[0]
    n = m // n_batch
    fw = F_GROUPS * F_GROUP_DIM
    tm = min(512, n)
    nt = n // tm
    return pl.pallas_call(
        _chan_dft_kernel,
        grid=(n_batch, nt),
        in_specs=[
            pl.BlockSpec((tm, fw), lambda b, t: (b * nt + t, _blk(f_off, fw))),
            pl.BlockSpec((F_GROUP_DIM, 2 * F_GROUP_DIM), lambda b, t: (0, 0)),
        ],
        out_specs=pl.BlockSpec((None, 2, tm, fw), lambda b, t: (b, 0, t, 0)),
        out_shape=jax.ShapeDtypeStruct((n_batch, 2, n, fw), BF16),
        compiler_params=_cparams(2),
        name="fourier_channel_dft",
    )(p, cs)


def _fourier_halves(p, cs, tables, *, n_batch, f_off):
    c_ext, s_ext, sel = tables
    nh, ext, n = c_ext.shape
    th = sel.shape[0]
    fw = F_GROUPS * F_GROUP_DIM
    z = _channel_dft(p, cs, n_batch=n_batch, f_off=f_off)
    tn = 512
    half = jax.ShapeDtypeStruct((n_batch * n // 2, fw), BF16)
    yd, ym = pl.pallas_call(
        _half_dft_kernel,
        grid=(nh, n_batch, fw // tn),
        in_specs=[
            pl.BlockSpec((None, ext, n), lambda j, b, c: (j, 0, 0)),
            pl.BlockSpec((None, ext, n), lambda j, b, c: (j, 0, 0)),
            pl.BlockSpec((th, ext), lambda j, b, c: (0, 0)),
            pl.BlockSpec((None, None, n, tn), lambda j, b, c: (b, 0, 0, c)),
            pl.BlockSpec((None, None, n, tn), lambda j, b, c: (b, 1, 0, c)),
        ],
        out_specs=[
            pl.BlockSpec((th, tn), lambda j, b, c: (b * nh + j, c)),
            pl.BlockSpec((th, tn), lambda j, b, c: (b * nh + (nh - 1 - j), c)),
        ],
        out_shape=[half, half],
        compiler_params=_cparams(3),
        name="fourier_position_half_dft",
    )(c_ext, s_ext, sel, z, z)
    return yd, ym


def _fourier(p, cs, dft, *, n_batch, f_off):
    m = p.shape[0]
    n = m // n_batch
    fw = F_GROUPS * F_GROUP_DIM
    tm = min(512, n)
    nt = n // tm
    z = _channel_dft(p, cs, n_batch=n_batch, f_off=f_off).reshape(n_batch, 2 * n, fw)
    tn = 512
    return pl.pallas_call(
        _mm_kernel,
        grid=(nt, n_batch, fw // tn),
        in_specs=[
            pl.BlockSpec((tm, 2 * n), lambda i, b, j: (i, 0)),
            pl.BlockSpec((None, 2 * n, tn), lambda i, b, j: (b, 0, j)),
        ],
        out_specs=pl.BlockSpec((tm, tn), lambda i, b, j: (b * nt + i, j)),
        out_shape=jax.ShapeDtypeStruct((m, fw), BF16),
        compiler_params=_cparams(3),
        name="fourier_position_dft",
    )(dft, z)


CONV_ROWS = 64


def _conv_kernel(a_ref, g_ref, ap_ref, gp_ref, an_ref, gn_ref, wb_ref, cb_ref, lg_ref, lb_ref,
                 o_ref, buf, shifted, conv_scr, *, ts):
    t = pl.program_id(1)
    last = pl.num_programs(1) - 1

    def glu(a, g):
        return a.astype(F32) * jax.nn.sigmoid(g.astype(F32))

    zeros = jnp.zeros((HALO, buf.shape[1]), F32)
    buf[0:HALO, :] = jnp.where(t > 0, glu(ap_ref[...], gp_ref[...]), zeros)
    buf[HALO:HALO + ts, :] = glu(a_ref[...], g_ref[...])
    buf[HALO + ts:2 * HALO + ts, :] = jnp.where(t < last, glu(an_ref[...], gn_ref[...]), zeros)

    span = ts + 3 * SUBLANES
    for s in range(1, SUBLANES):
        shifted[s - 1, 0:span, :] = buf[s:s + span, :]

    n_lane_chunks = buf.shape[1] // LANES
    groups = CONV_ROWS // SUBLANES

    def body(rc, carry):
        r0 = pl.multiple_of(rc * CONV_ROWS, CONV_ROWS)
        for lc in range(n_lane_chunks):
            ls = slice(lc * LANES, (lc + 1) * LANES)
            accs = [jnp.zeros((SUBLANES, LANES), F32) for _ in range(groups)]
            for k in range(CONV_K):
                whole, part = divmod(k + HALO - CONV_K // 2, SUBLANES)
                w = wb_ref[k * SUBLANES:(k + 1) * SUBLANES, ls]
                for j in range(groups):
                    row = r0 + (whole + j) * SUBLANES
                    if part == 0:
                        x = buf[pl.ds(row, SUBLANES), ls]
                    else:
                        x = shifted[part - 1, pl.ds(row, SUBLANES), ls]
                    accs[j] = accs[j] + w * x
            bias = cb_ref[:, ls]
            for j in range(groups):
                conv_scr[pl.ds(r0 + j * SUBLANES, SUBLANES), ls] = accs[j] + bias
        return carry

    lax.fori_loop(0, ts // CONV_ROWS, body, 0)

    v = conv_scr[...]
    mu = jnp.mean(v, axis=-1, keepdims=True)
    dv = v - mu
    var = jnp.mean(dv * dv, axis=-1, keepdims=True)
    y = dv * lax.rsqrt(var + EPS) * lg_ref[...] + lb_ref[...]
    o_ref[...] = _silu(y).astype(o_ref.dtype)


def _conv(p, wb, cb, lg, lb, *, n_batch, a_off, g_off):
    m = p.shape[0]
    cw = cb.shape[0]
    n = m // n_batch
    ts = min(256, n)
    nt = n // ts
    halo_per_tile = ts // HALO
    last_halo = m // HALO - 1

    def main(off):
        return pl.BlockSpec((ts, cw), lambda b, t: (b * nt + t, _blk(off, cw)))

    def prev(off):
        return pl.BlockSpec(
            (HALO, cw), lambda b, t: (jnp.maximum((b * nt + t) * halo_per_tile - 1, 0), _blk(off, cw)))

    def nxt(off):
        return pl.BlockSpec(
            (HALO, cw),
            lambda b, t: (jnp.minimum((b * nt + t + 1) * halo_per_tile, last_halo), _blk(off, cw)))

    vec = pl.BlockSpec((1, cw), lambda b, t: (0, 0))
    return pl.pallas_call(
        functools.partial(_conv_kernel, ts=ts),
        grid=(n_batch, nt),
        in_specs=[
            main(a_off), main(g_off), prev(a_off), prev(g_off), nxt(a_off), nxt(g_off),
            pl.BlockSpec((CONV_K * SUBLANES, cw), lambda b, t: (0, 0)),
            vec, vec, vec,
        ],
        out_specs=pl.BlockSpec((ts, cw), lambda b, t: (b * nt + t, 0)),
        out_shape=jax.ShapeDtypeStruct((m, cw), BF16),
        scratch_shapes=[
            pltpu.VMEM((ts + 2 * HALO, cw), F32),
            pltpu.VMEM((SUBLANES - 1, ts + 3 * SUBLANES, cw), F32),
            pltpu.VMEM((ts, cw), F32),
        ],
        compiler_params=_cparams(2),
        name="conformer_conv",
    )(p, p, p, p, p, p, wb, cb.reshape(1, cw), lg.reshape(1, cw), lb.reshape(1, cw))


def _merge_kernel(yd_ref, ym_ref, cv_ref, oa_ref, gf_ref, gc_ref, ga_ref, bf_ref, bc_ref, ba_ref,
                  wf_ref, wc_ref, wa_ref, o_ref, *, tiles_per_batch):
    def gate(g_ref, b_ref):
        return jax.nn.sigmoid(g_ref[...].astype(F32) + b_ref[...])

    if tiles_per_batch:
        first = (pl.program_id(0) % tiles_per_batch) < tiles_per_batch // 2
        yf = jnp.where(first, yd_ref[...], ym_ref[...])
    else:
        yf = yd_ref[...]
    acc = gate(gf_ref, bf_ref) * _dot(yf, wf_ref[...])
    acc = acc + gate(gc_ref, bc_ref) * _dot(cv_ref[...], wc_ref[...])
    acc = acc + gate(ga_ref, ba_ref) * _dot(oa_ref[...], wa_ref[...])
    o_ref[...] = acc.astype(o_ref.dtype)


def _merge(yf, cv, oa, p, b_gate, wf, wc, wa, layer, *, gate_off, rows_per_batch=None):
    m = cv.shape[0]
    d = wf.shape[2]
    tn = 512
    nj = d // tn
    g0 = _blk(gate_off, tn)
    if isinstance(yf, tuple):
        yd, ym = yf
        tm = min(1024, rows_per_batch // 2)
        tiles_per_batch = rows_per_batch // tm
        hpb = tiles_per_batch // 2

        def half_tile(i):
            return (i // tiles_per_batch) * hpb, i % tiles_per_batch

        def yd_map(i, j):
            base, it = half_tile(i)
            return base + jnp.minimum(it, hpb - 1), 0

        def ym_map(i, j):
            base, it = half_tile(i)
            return base + jnp.maximum(it - hpb, 0), 0

        y_specs = [pl.BlockSpec((tm, yd.shape[1]), yd_map), pl.BlockSpec((tm, ym.shape[1]), ym_map)]
    else:
        yd = ym = yf
        tm = min(1024, m)
        tiles_per_batch = 0
        y_specs = [pl.BlockSpec((tm, yf.shape[1]), lambda i, j: (i, 0))] * 2

    def act(a):
        return pl.BlockSpec((tm, a.shape[1]), lambda i, j: (i, 0))

    def gate(br):
        return pl.BlockSpec((tm, tn), lambda i, j: (i, g0 + br * nj + j))

    def bias(br):
        return pl.BlockSpec((None, 1, tn), lambda i, j: (layer, 0, br * nj + j))

    def weight(w):
        return _wspec(w, layer, tn)

    return pl.pallas_call(
        functools.partial(_merge_kernel, tiles_per_batch=tiles_per_batch),
        grid=(m // tm, nj),
        in_specs=y_specs + [act(cv), act(oa), gate(0), gate(1), gate(2),
                            bias(0), bias(1), bias(2), weight(wf), weight(wc), weight(wa)],
        out_specs=pl.BlockSpec((tm, tn), lambda i, j: (i, j)),
        out_shape=jax.ShapeDtypeStruct((m, d), BF16),
        compiler_params=_cparams(2),
        name="branch_merge",
    )(yd, ym, cv, oa, p, p, p, b_gate, b_gate, b_gate, wf, wc, wa)


def _final_norm_kernel(x_ref, g_ref, o_ref):
    o_ref[...] = _rms(x_ref[...], g_ref[...])


def _final_norm(x, g):
    m, d = x.shape
    tm = min(512, m)
    return pl.pallas_call(
        _final_norm_kernel,
        grid=(m // tm,),
        in_specs=[pl.BlockSpec((tm, d), lambda i: (i, 0)), pl.BlockSpec((1, d), lambda i: (0, 0))],
        out_specs=pl.BlockSpec((tm, d), lambda i: (i, 0)),
        out_shape=jax.ShapeDtypeStruct((m, d), F32),
        compiler_params=_cparams(1),
        name="final_rmsnorm",
    )(x, g.reshape(1, d))


def _rope_tables(n_tok):
    rows = n_tok // GRID_W
    row = np.repeat(np.arange(rows, dtype=np.float32), GRID_W)
    col = np.tile(np.arange(GRID_W, dtype=np.float32), rows)
    n_freq = QK_ROPE // 4
    inv_freq = (np.float32(ROPE_BASE) ** (-np.arange(n_freq, dtype=np.float32) / n_freq)).astype(np.float32)
    ang = np.concatenate([row[:, None] * inv_freq, col[:, None] * inv_freq], axis=-1)
    pad = np.zeros((n_tok, LANES - QK_ROPE), np.float32)
    cos = np.concatenate([np.cos(ang), np.cos(ang), pad], axis=-1).astype(np.float32)
    sin = np.concatenate([np.sin(ang), np.sin(ang), pad], axis=-1).astype(np.float32)
    return cos, sin


def _dft_cos_sin(n, k=None):
    k = np.arange(n) if k is None else np.asarray(k)
    ang = ((k[:, None] * np.arange(n)[None, :]) % n) * (2.0 * np.pi / n)
    scale = n ** -0.5
    return np.cos(ang) * scale, np.sin(ang) * scale


DFT_HALF_TILE = 512


def _half_dft_tables(n):
    th = min(DFT_HALF_TILE, n // 2)
    nh = n // 2 // th
    ext = th + BF16_ROWS
    k = np.arange(nh)[:, None] * th + np.arange(ext)[None, :]
    c, s = _dft_cos_sin(n, k.reshape(-1))
    sel = np.arange(ext)[None, :] == th - np.arange(th)[:, None]
    return tuple(jnp.asarray(a.astype(BF16))
                 for a in (c.reshape(nh, ext, n), s.reshape(nh, ext, n), sel))


def kernel(x, c, ctx, c_ctx, ada_w, ada_b, norm_mix_g, w_in, b_gate, w_fourier, conv_w, conv_b,
           conv_ln_g, conv_ln_b, w_conv_out, q_norm_g, w_uq, kv_norm_g, w_ukv, w_mla_o, w_out,
           norm_ffn_g, w_ffn_gate, w_ffn_up, w_ffn_down, final_norm_g):
    n_batch, seq, d = x.shape
    n_ctx = ctx.shape[1]
    depth = ada_w.shape[0]
    assert n_batch < SUBLANES
    q_lora = q_norm_g.shape[1]
    kv_lora = kv_norm_g.shape[1]
    f_width = w_fourier.shape[1]
    c_width = conv_w.shape[2]
    assert (f_width, c_width, q_lora, kv_lora) == (PX_A - PX_F, PX_G - PX_A, PX_GATE - PX_QD,
                                                    PX_KPE - PX_KVD)
    f_end = f_width
    c_end = f_end + 2 * c_width
    q_end = c_end + q_lora
    kv_end = q_end + kv_lora
    kpe_end = kv_end + QK_ROPE

    w_in_r = _relayout_rows(jnp.swapaxes(w_in, 1, 2), PX_COLS, name="w_in_relayout", moves=(
        (PX_F, 0, c_end),
        (PX_KVD, q_end, kv_lora),
        (PX_KPE, kv_end, QK_ROPE),
        (PX_QD, c_end, q_lora),
        (PX_GATE, kpe_end, w_in.shape[2] - kpe_end),
    ), zero_span=(PX_KPE, PX_QD - PX_KPE))
    qk = QK_NOPE + QK_ROPE
    w_uq_r = _relayout_cols(
        w_uq, N_HEADS * HEAD_W, name="w_uq_relayout",
        moves=tuple((h * HEAD_W, h * qk, qk) for h in range(N_HEADS)),
        zero_span=(0, N_HEADS * HEAD_W))
    w_ukv_r = _relayout_cols(
        w_ukv, 2 * N_HEADS * QK_NOPE, name="w_ukv_relayout",
        moves=tuple(((s * N_HEADS + h) * QK_NOPE, (2 * h + s) * QK_NOPE, QK_NOPE)
                    for h in range(N_HEADS) for s in range(2)),
        zero_span=(0, 0))
    w_fourier_b = w_fourier.astype(BF16)
    w_conv_out_b = w_conv_out.astype(BF16)
    w_mla_o_b = w_mla_o.astype(BF16)
    w_out_b = w_out.astype(BF16)
    w_gate_b = w_ffn_gate.astype(BF16)
    w_up_b = w_ffn_up.astype(BF16)
    w_down_b = w_ffn_down.astype(BF16)
    conv_wb = jnp.repeat(conv_w, SUBLANES, axis=1)
    b_gate3 = b_gate.reshape(depth, 1, -1)

    cos_l, sin_l = (jnp.asarray(t) for t in _rope_tables(seq))
    cos_c = jnp.ones((n_batch * n_ctx, LANES), F32)
    sin_c = jnp.zeros((n_batch * n_ctx, LANES), F32)
    c128, s128 = _dft_cos_sin(F_GROUP_DIM)
    cs = jnp.asarray(np.concatenate([c128, s128], axis=1).astype(BF16))
    half_dft_l = _half_dft_tables(seq)
    cc, sc = _dft_cos_sin(n_ctx)
    dft_c = jnp.asarray(np.concatenate([cc, -sc], axis=1).astype(BF16))

    c8 = jnp.concatenate(
        [c, c_ctx[None, :], jnp.zeros((SUBLANES - n_batch - 1, d), c.dtype)], axis=0)
    mod_all = _ada(c8, ada_w, ada_b).reshape(depth, SUBLANES, 1, 6 * d)

    lat = _Rows(n_batch * seq, seq, None)
    cxt = _Rows(n_batch * n_ctx, None, n_batch)
    xs = x.reshape(n_batch * seq, d)
    cs_rows = ctx.reshape(n_batch * n_ctx, d)

    def mix_and_ffn(i, rows, h_in, p, o_attn, nb):
        mod3 = mod_all[i]
        if rows is lat:
            yf = _fourier_halves(p, cs, half_dft_l, n_batch=nb, f_off=PX_F)
        else:
            yf = _fourier(p, cs, dft_c, n_batch=nb, f_off=PX_F)
        cv = _conv(p, conv_wb[i], conv_b[i], conv_ln_g[i], conv_ln_b[i],
                   n_batch=nb, a_off=PX_A, g_off=PX_G)
        merged = _merge(yf, cv, o_attn, p, b_gate3, w_fourier_b, w_conv_out_b, w_mla_o_b, i,
                        gate_off=PX_GATE, rows_per_batch=rows.rows_per_batch)
        mixed = _resmm(merged, w_out_b, i, h_in, mod3, rows, gate_chunk=2, tm=2048, tn=512)
        hid = _ffn_up(mixed, norm_ffn_g[i], mod3, w_gate_b, w_up_b, i, rows)
        return _resmm(hid, w_down_b, i, mixed, mod3, rows, gate_chunk=5, tm=1024, tn=512)

    for i in range(depth):
        last = i == depth - 1
        mod3 = mod_all[i]
        px = _inproj(xs, norm_mix_g[i], mod3, w_in_r, i, lat, shift_chunk=0, tn=1536)
        if last:
            pc = _inproj(cs_rows, norm_mix_g[i], mod3, w_in_r, i, cxt, shift_chunk=0,
                         tn=PKV_COLS, col_off=PX_KVD, n=PKV_COLS)
            kvd_off, kpe_off = PKV_KVD, PKV_KPE
        else:
            pc = _inproj(cs_rows, norm_mix_g[i], mod3, w_in_r, i, cxt, shift_chunk=0, tn=1536)
            kvd_off, kpe_off = PX_KVD, PX_KPE

        q_x = _qup(px, q_norm_g[i], cos_l, sin_l, w_uq_r, i, qd_off=PX_QD)
        kv_x, kpe_x = _kvup(px, kv_norm_g[i], cos_l, sin_l, w_ukv_r, i,
                            kvd_off=PX_KVD, kpe_off=PX_KPE)
        kv_c, kpe_c = _kvup(pc, kv_norm_g[i], cos_c, sin_c, w_ukv_r, i,
                            kvd_off=kvd_off, kpe_off=kpe_off)
        o_x = _attention(q_x, kv_c, kpe_c, n_batch=n_batch, n_ctx=n_ctx,
                         q_rows_per_batch=seq, lat=(kv_x, kpe_x))
        if not last:
            q_c = _qup(pc, q_norm_g[i], cos_c, sin_c, w_uq_r, i, qd_off=PX_QD)
            o_c = _attention(q_c, kv_c, kpe_c, n_batch=n_batch, n_ctx=n_ctx,
                             q_rows_per_batch=n_ctx)
            cs_rows = mix_and_ffn(i, cxt, cs_rows, pc, o_c, n_batch)
        xs = mix_and_ffn(i, lat, xs, px, o_x, n_batch)

    return _final_norm(xs, final_norm_g).reshape(n_batch, seq, d)
```
